```python
import math
import jax, jax.numpy as jnp
from jax import lax
import numpy as np

D_MODEL = 1024
BATCH = 4
SEQ = 8192
DEPTH = 2

N_MIXERS = 2
N_DIFF_LAYERS = (DEPTH + 1) // 2
N_NSA_LAYERS = DEPTH // 2

DEEPNORM_ALPHA = (2 * DEPTH) ** 0.25
DEEPNORM_BETA = (8 * DEPTH) ** -0.25
LN_EPS = 1e-5
RMS_EPS = 1e-5
NEG_INF = -1e30
FORCED_SCORE = 1e6

DIFF_HEADS = 8
DIFF_HEAD_DIM = 64
DIFF_V_DIM = 2 * DIFF_HEAD_DIM
DIFF_Q_BLOCK = 128

NSA_HEADS = 16
NSA_GROUPS = 4
NSA_HEAD_DIM = 64
NSA_CMP_LEN = 32
NSA_CMP_STRIDE = 16
NSA_CMP_HIDDEN = 128
NSA_SLC_LEN = 64
NSA_SLC_TOPK = 16
NSA_WINDOW = 512
NSA_Q_BLOCK = 64

FFN_HIDDEN = -(-8 * D_MODEL // (3 * 256)) * 256

kernel_name = 'hybrid_diffattn_nsa_deepnorm_adaln'


def layer_norm(x, g, b):
    xf = x.astype(jnp.float32)
    mu = jnp.mean(xf, axis=-1, keepdims=True)
    var = jnp.mean(jnp.square(xf - mu), axis=-1, keepdims=True)
    return ((xf - mu) * lax.rsqrt(var + LN_EPS) * g.astype(jnp.float32) + b.astype(jnp.float32)).astype(x.dtype)


def rms_norm(x, g):
    xf = x.astype(jnp.float32)
    y = xf * lax.rsqrt(jnp.mean(jnp.square(xf), axis=-1, keepdims=True) + RMS_EPS)
    return (y * g.astype(jnp.float32)).astype(x.dtype)


def alibi_slopes(n_heads):
    return 2.0 ** (-8.0 * jnp.arange(1, n_heads + 1, dtype=jnp.float32) / n_heads)


def masked_softmax(s, mask):
    p = jax.nn.softmax(jnp.where(mask, s, NEG_INF), axis=-1)
    return jnp.where(mask, p, 0.0)


def diff_attention(h, w_in, w_out, lam_q1, lam_k1, lam_q2, lam_k2, subln_g, layer_idx):
    B, T, _ = h.shape
    H, dh = DIFF_HEADS, DIFF_HEAD_DIM
    q, k, v = jnp.split(h @ w_in, 3, axis=-1)
    q = q.reshape(B, T, H, 2, dh)
    k = k.reshape(B, T, H, 2, dh)
    v = v.reshape(B, T, H, DIFF_V_DIM)
    lam_init = 0.8 - 0.6 * math.exp(-0.3 * layer_idx)
    f32 = jnp.float32
    lam = (jnp.exp(jnp.sum(lam_q1.astype(f32) * lam_k1.astype(f32)))
           - jnp.exp(jnp.sum(lam_q2.astype(f32) * lam_k2.astype(f32))) + lam_init)
    slopes = alibi_slopes(H)[:, None, None, None]
    scale = dh ** -0.5
    outs = []
    for qb in range(T // DIFF_Q_BLOCK):
        start = qb * DIFF_Q_BLOCK
        end = start + DIFF_Q_BLOCK
        qi = q[:, start:end]
        kj = k[:, :end]
        vj = v[:, :end]
        dist = (jnp.arange(start, end)[:, None] - jnp.arange(end)[None, :]).astype(f32)
        s = jnp.einsum('bqhcd,bkhcd->bhcqk', qi, kj).astype(f32) * scale - slopes * dist
        p = masked_softmax(s, dist >= 0)
        a = p[:, :, 0] - lam * p[:, :, 1]
        outs.append(jnp.einsum('bhqk,bkhe->bqhe', a.astype(vj.dtype), vj))
    o = jnp.concatenate(outs, axis=1)
    o = rms_norm(o, subln_g) * (1.0 - lam_init)
    return o.reshape(B, T, H * DIFF_V_DIM) @ w_out


def compress_blocks(x, win_idx, pos, w1, w2):
    B, _, G, d = x.shape
    n_cmp, L = win_idx.shape
    blk = x[:, win_idx] + pos[:, None, :]
    blk = blk.transpose(0, 1, 3, 2, 4).reshape(B, n_cmp, G, L * d)
    return jax.nn.gelu(blk @ w1) @ w2


def cmp_to_slc_weights(n_cmp, n_slc):
    c_start = jnp.arange(n_cmp) * NSA_CMP_STRIDE
    s_start = jnp.arange(n_slc) * NSA_SLC_LEN
    ov = (jnp.minimum(c_start[:, None] + NSA_CMP_LEN, s_start[None, :] + NSA_SLC_LEN)
          - jnp.maximum(c_start[:, None], s_start[None, :]))
    return jnp.clip(ov, 0).astype(jnp.float32) / NSA_CMP_LEN


def nsa_attention(h, w_in, cmp_pos_k, cmp_pos_v, cmp_k_w1, cmp_k_w2, cmp_v_w1, cmp_v_w2, w_out):
    B, T, _ = h.shape
    H, G, d = NSA_HEADS, NSA_GROUPS, NSA_HEAD_DIM
    hpg = H // G
    QB, W, LS = NSA_Q_BLOCK, NSA_WINDOW, NSA_SLC_LEN
    f32 = jnp.float32
    sizes = [H * d] + [G * d] * 6 + [H * 3]
    parts = jnp.split(h @ w_in, np.cumsum(sizes)[:-1].tolist(), axis=-1)
    q = parts[0].reshape(B, T, G, hpg, d)
    k_c, v_c, k_s, v_s, k_w, v_w = [p.reshape(B, T, G, d) for p in parts[1:7]]
    gates = jax.nn.sigmoid(parts[7].astype(f32)).astype(h.dtype).reshape(B, T, G, hpg, 3)

    n_cmp = (T - NSA_CMP_LEN) // NSA_CMP_STRIDE + 1
    n_slc = T // LS
    k_sel = min(NSA_SLC_TOPK, n_slc)
    win_idx = jnp.arange(n_cmp)[:, None] * NSA_CMP_STRIDE + jnp.arange(NSA_CMP_LEN)[None, :]
    kc = compress_blocks(k_c, win_idx, cmp_pos_k, cmp_k_w1, cmp_k_w2)
    vc = compress_blocks(v_c, win_idx, cmp_pos_v, cmp_v_w1, cmp_v_w2)
    kpos_c = win_idx[:, -1]
    overlap = cmp_to_slc_weights(n_cmp, n_slc)
    ksb = k_s.reshape(B, n_slc, LS, G, d).transpose(0, 3, 1, 2, 4)
    vsb = v_s.reshape(B, n_slc, LS, G, d).transpose(0, 3, 1, 2, 4)
    kw_pad = jnp.pad(k_w, ((0, 0), (W, 0), (0, 0), (0, 0)))
    vw_pad = jnp.pad(v_w, ((0, 0), (W, 0), (0, 0), (0, 0)))
    slopes = alibi_slopes(H).reshape(G, hpg)
    scale = d ** -0.5
    b_ix = jnp.arange(B)[:, None, None, None]
    g_ix = jnp.arange(G)[None, :, None, None]
    blk_ids = jnp.arange(n_slc)

    def one_block(qb):
        start = qb * QB
        qi = lax.dynamic_slice_in_dim(q, start, QB, axis=1)
        gi = lax.dynamic_slice_in_dim(gates, start, QB, axis=1)
        qpos = start + jnp.arange(QB)
        dist_c = (qpos[:, None] - kpos_c[None, :]).astype(f32)
        s_c = (jnp.einsum('bqghd,bngd->bghqn', qi, kc).astype(f32) * scale
               - slopes[:, :, None, None] * dist_c)
        p_c = masked_softmax(s_c, dist_c >= 0)
        o_c = jnp.einsum('bghqn,bngd->bqghd', p_c.astype(vc.dtype), vc)
        imp = jnp.einsum('bghqn,nj->bgqj', p_c, overlap)
        q_blk = qpos // LS
        forced = ((blk_ids[None, :] == 0) | (blk_ids[None, :] == q_blk[:, None])
                  | (blk_ids[None, :] == q_blk[:, None] - 1))
        causal_blk = blk_ids[None, :] * LS <= qpos[:, None]
        imp = jnp.where(causal_blk, jnp.where(forced, FORCED_SCORE, imp), NEG_INF)
        _, top_idx = lax.top_k(imp, k_sel)
        ksel = ksb[b_ix, g_ix, top_idx]
        vsel = vsb[b_ix, g_ix, top_idx].reshape(B, G, QB, k_sel * LS, d)
        kpos_s = top_idx[..., None] * LS + jnp.arange(LS)
        dist_s = (qpos[None, None, :, None, None] - kpos_s).astype(f32).reshape(B, G, 1, QB, k_sel * LS)
        s_s = (jnp.einsum('bqghd,bgqkld->bghqkl', qi, ksel).astype(f32).reshape(B, G, hpg, QB, k_sel * LS)
               * scale - slopes[None, :, :, None, None] * dist_s)
        p_s = masked_softmax(s_s, dist_s >= 0)
        o_s = jnp.einsum('bghqm,bgqmd->bqghd', p_s.astype(vsel.dtype), vsel)
        kw = lax.dynamic_slice_in_dim(kw_pad, start, W + QB, axis=1)
        vw = lax.dynamic_slice_in_dim(vw_pad, start, W + QB, axis=1)
        kpos_w = start - W + jnp.arange(W + QB)
        dist_w = qpos[:, None] - kpos_w[None, :]
        mask_w = (dist_w >= 0) & (dist_w < W) & (kpos_w[None, :] >= 0)
        s_w = (jnp.einsum('bqghd,bkgd->bghqk', qi, kw).astype(f32) * scale
               - slopes[:, :, None, None] * dist_w.astype(f32))
        p_w = masked_softmax(s_w, mask_w)
        o_w = jnp.einsum('bghqk,bkgd->bqghd', p_w.astype(vw.dtype), vw)
        o = gi[..., 0:1] * o_c + gi[..., 1:2] * o_s + gi[..., 2:3] * o_w
        return o.reshape(B, QB, H * d)

    o = lax.map(one_block, jnp.arange(T // QB))
    o = o.transpose(1, 0, 2, 3).reshape(B, T, H * d)
    return o @ w_out


def swiglu(h, w_gu, w_down):
    g, u = jnp.split(h @ w_gu, 2, axis=-1)
    return (jax.nn.silu(g) * u) @ w_down


def setup_inputs(seed: int = 0) -> dict:
    key = jax.random.key(seed)
    ks = iter(jax.random.split(key, 32))
    f32 = jnp.float32

    def nrm(shape, scale):
        return jax.random.normal(next(ks), shape, f32) * scale

    D, beta = D_MODEL, DEEPNORM_BETA
    Hd, dh = DIFF_HEADS, DIFF_HEAD_DIM
    Hn, G, d = NSA_HEADS, NSA_GROUPS, NSA_HEAD_DIM
    Lc = NSA_CMP_LEN
    x = nrm((BATCH, SEQ, D), 1.0)
    c = nrm((BATCH, D), 1.0)
    diff_w_in = jnp.concatenate([nrm((N_DIFF_LAYERS, D, 2 * Hd * dh), D ** -0.5),
                                 nrm((N_DIFF_LAYERS, D, 2 * Hd * dh), D ** -0.5),
                                 nrm((N_DIFF_LAYERS, D, Hd * DIFF_V_DIM), beta * D ** -0.5)], axis=-1)
    diff_w_out = nrm((N_DIFF_LAYERS, Hd * DIFF_V_DIM, D), beta * (Hd * DIFF_V_DIM) ** -0.5)
    diff_lam_q1 = nrm((N_DIFF_LAYERS, dh), 0.1)
    diff_lam_k1 = nrm((N_DIFF_LAYERS, dh), 0.1)
    diff_lam_q2 = nrm((N_DIFF_LAYERS, dh), 0.1)
    diff_lam_k2 = nrm((N_DIFF_LAYERS, dh), 0.1)
    diff_subln_g = 1.0 + nrm((N_DIFF_LAYERS, DIFF_V_DIM), 0.02)
    col_scale = jnp.concatenate([jnp.ones((Hn * d,), f32)]
                                + [jnp.full((G * d,), s, f32) for s in (1.0, beta, 1.0, beta, 1.0, beta)]
                                + [jnp.ones((Hn * 3,), f32)])
    nsa_w_in = nrm((N_NSA_LAYERS, D, col_scale.shape[0]), D ** -0.5) * col_scale
    nsa_cmp_pos_k = nrm((N_NSA_LAYERS, Lc, d), 0.1)
    nsa_cmp_pos_v = nrm((N_NSA_LAYERS, Lc, d), 0.1)
    nsa_cmp_k_w1 = nrm((N_NSA_LAYERS, Lc * d, NSA_CMP_HIDDEN), (Lc * d) ** -0.5)
    nsa_cmp_k_w2 = nrm((N_NSA_LAYERS, NSA_CMP_HIDDEN, d), NSA_CMP_HIDDEN ** -0.5)
    nsa_cmp_v_w1 = nrm((N_NSA_LAYERS, Lc * d, NSA_CMP_HIDDEN), (Lc * d) ** -0.5)
    nsa_cmp_v_w2 = nrm((N_NSA_LAYERS, NSA_CMP_HIDDEN, d), NSA_CMP_HIDDEN ** -0.5)
    nsa_w_out = nrm((N_NSA_LAYERS, Hn * d, D), beta * (Hn * d) ** -0.5)
    ada_w = nrm((DEPTH, D, 6 * D), 0.1 * D ** -0.5)
    ada_b = nrm((DEPTH, 6 * D), 0.01)
    ln_mix_g = 1.0 + nrm((DEPTH, D), 0.02)
    ln_mix_b = nrm((DEPTH, D), 0.02)
    ln_ffn_g = 1.0 + nrm((DEPTH, D), 0.02)
    ln_ffn_b = nrm((DEPTH, D), 0.02)
    ffn_w_gu = nrm((DEPTH, D, 2 * FFN_HIDDEN), beta * D ** -0.5)
    ffn_w_down = nrm((DEPTH, FFN_HIDDEN, D), beta * FFN_HIDDEN ** -0.5)
    return {'x': x, 'c': c,
            'diff_w_in': diff_w_in, 'diff_w_out': diff_w_out,
            'diff_lam_q1': diff_lam_q1, 'diff_lam_k1': diff_lam_k1,
            'diff_lam_q2': diff_lam_q2, 'diff_lam_k2': diff_lam_k2, 'diff_subln_g': diff_subln_g,
            'nsa_w_in': nsa_w_in, 'nsa_cmp_pos_k': nsa_cmp_pos_k, 'nsa_cmp_pos_v': nsa_cmp_pos_v,
            'nsa_cmp_k_w1': nsa_cmp_k_w1, 'nsa_cmp_k_w2': nsa_cmp_k_w2,
            'nsa_cmp_v_w1': nsa_cmp_v_w1, 'nsa_cmp_v_w2': nsa_cmp_v_w2, 'nsa_w_out': nsa_w_out,
            'ada_w': ada_w, 'ada_b': ada_b,
            'ln_mix_g': ln_mix_g, 'ln_mix_b': ln_mix_b, 'ln_ffn_g': ln_ffn_g, 'ln_ffn_b': ln_ffn_b,
            'ffn_w_gu': ffn_w_gu, 'ffn_w_down': ffn_w_down}


def reference(x, c, diff_w_in, diff_w_out, diff_lam_q1, diff_lam_k1, diff_lam_q2, diff_lam_k2,
              diff_subln_g, nsa_w_in, nsa_cmp_pos_k, nsa_cmp_pos_v, nsa_cmp_k_w1, nsa_cmp_k_w2,
              nsa_cmp_v_w1, nsa_cmp_v_w2, nsa_w_out, ada_w, ada_b, ln_mix_g, ln_mix_b,
              ln_ffn_g, ln_ffn_b, ffn_w_gu, ffn_w_down):
    for i in range(DEPTH):
        mod = jax.nn.silu(c) @ ada_w[i] + ada_b[i]
        sh_a, sc_a, g_a, sh_f, sc_f, g_f = jnp.split(mod[:, None, :], 6, axis=-1)
        h = x * (1.0 + sc_a) + sh_a
        j = i // N_MIXERS
        if i % N_MIXERS == 0:
            y = diff_attention(h, diff_w_in[j], diff_w_out[j], diff_lam_q1[j], diff_lam_k1[j],
                               diff_lam_q2[j], diff_lam_k2[j], diff_subln_g[j], i)
        else:
            y = nsa_attention(h, nsa_w_in[j], nsa_cmp_pos_k[j], nsa_cmp_pos_v[j], nsa_cmp_k_w1[j],
                              nsa_cmp_k_w2[j], nsa_cmp_v_w1[j], nsa_cmp_v_w2[j], nsa_w_out[j])
        x = layer_norm(DEEPNORM_ALPHA * x + (1.0 + g_a) * y, ln_mix_g[i], ln_mix_b[i])
        h = x * (1.0 + sc_f) + sh_f
        x = layer_norm(DEEPNORM_ALPHA * x + (1.0 + g_f) * swiglu(h, ffn_w_gu[i], ffn_w_down[i]),
                       ln_ffn_g[i], ln_ffn_b[i])
    return x
```

```python
import functools
import math

import numpy as np
import jax
import jax.numpy as jnp
from jax import lax
from jax.experimental import pallas as pl
from jax.experimental.pallas import tpu as pltpu

F32 = jnp.float32
BF16 = jnp.bfloat16

D_MODEL = 1024
DEPTH = 2
N_MIXERS = 2
DEEPNORM_ALPHA = (2 * DEPTH) ** 0.25
LN_EPS = 1e-5
RMS_EPS = 1e-5
NEG_INF = -1e30
FORCED_SCORE = 1e6

DIFF_HEADS = 8
DIFF_HEAD_DIM = 64
DIFF_V_DIM = 2 * DIFF_HEAD_DIM

NSA_HEADS = 16
NSA_GROUPS = 4
NSA_HEAD_DIM = 64
NSA_HPG = NSA_HEADS // NSA_GROUPS
NSA_CMP_LEN = 32
NSA_CMP_STRIDE = 16
NSA_CMP_HIDDEN = 128
NSA_SLC_LEN = 64
NSA_SLC_TOPK = 16
NSA_WINDOW = 512

FFN_HIDDEN = -(-8 * D_MODEL // (3 * 256)) * 256

LANES = 128
VMEM_LIMIT = 56 * 1024 * 1024
PROJ_ROWS = 512
FFN_CHUNK = FFN_HIDDEN // 2
DIFF_TILE = 256
CMP_Q_TILE = 256
SEL_Q_TILE = 128
SEL_K_TILE = 256
M_FLOOR = -1e29


def _cparams(sem):
    return pltpu.CompilerParams(dimension_semantics=sem, vmem_limit_bytes=VMEM_LIMIT)


def _dot(a, b):
    return jnp.dot(a, b, preferred_element_type=F32)


def _dot_nt(a, b):
    return lax.dot_general(a, b, (((1,), (1,)), ((), ())), preferred_element_type=F32)


def _layer_norm(z, g, b):
    mu = jnp.mean(z, axis=-1, keepdims=True)
    zc = z - mu
    var = jnp.mean(zc * zc, axis=-1, keepdims=True)
    return zc * lax.rsqrt(var + LN_EPS) * g + b


def _alibi_slopes(n_heads):
    return (2.0 ** (-8.0 * np.arange(1, n_heads + 1, dtype=np.float32) / n_heads)).astype(np.float32)


def _ada_kernel(c_ref, w_ref, b_ref, o_ref):
    c = c_ref[...]
    s = (c * jax.nn.sigmoid(c)).astype(BF16)
    o_ref[0] = _dot(s, w_ref[0].astype(BF16)) + b_ref[0]


def _ada_modulation(c, ada_w, ada_b):
    depth, d, d6 = ada_w.shape
    b = c.shape[0]
    rows = 8
    c_pad = jnp.zeros((rows, d), F32).at[:b].set(c)
    out = pl.pallas_call(
        _ada_kernel,
        out_shape=jax.ShapeDtypeStruct((depth, rows, d6), F32),
        grid=(depth, d6 // d),
        in_specs=[
            pl.BlockSpec((rows, d), lambda l, j: (0, 0)),
            pl.BlockSpec((1, d, d), lambda l, j: (l, 0, j)),
            pl.BlockSpec((1, 1, d), lambda l, j: (l, 0, j)),
        ],
        out_specs=pl.BlockSpec((1, rows, d), lambda l, j: (l, 0, j)),
        compiler_params=_cparams(("arbitrary", "arbitrary")),
        name="ada_modulation",
    )(c_pad, ada_w, ada_b.reshape(depth, 1, d6))
    return out[:, :b].reshape(depth, b, 6, 1, d)


def _mod_spec(layer, which, n_grid):
    d = D_MODEL
    if n_grid == 2:
        return pl.BlockSpec((None, None, None, 1, d), lambda b, i: (layer, b, which, 0, 0))
    return pl.BlockSpec((None, None, None, 1, d), lambda b, i, k: (layer, b, which, 0, 0))


def _proj_kernel(x_ref, sc_ref, sh_ref, w_ref, *out_refs, plan):
    h = (x_ref[...] * (1.0 + sc_ref[...]) + sh_ref[...]).astype(BF16)
    for ref, (c0, c1, kind), in zip(out_refs, plan):
        step = 1024
        for s0 in range(c0, c1, step):
            s1 = min(s0 + step, c1)
            y = _dot(h, w_ref[:, s0:s1])
            if kind == "sigmoid":
                y = jax.nn.sigmoid(y)
            ref[:, s0 - c0:s1 - c0] = y.astype(ref.dtype)


def _modulated_projection(x, mod, layer, w, plan, out_dtypes):
    b, t, d = x.shape
    n = w.shape[1]
    tm = min(PROJ_ROWS, t)
    out_shape = [jax.ShapeDtypeStruct((b, t, c1 - c0), dt) for (c0, c1, _), dt in zip(plan, out_dtypes)]
    out_specs = [pl.BlockSpec((None, tm, c1 - c0), lambda bb, i: (bb, i, 0)) for (c0, c1, _) in plan]
    return pl.pallas_call(
        functools.partial(_proj_kernel, plan=plan),
        out_shape=out_shape,
        grid=(b, t // tm),
        in_specs=[
            pl.BlockSpec((None, tm, d), lambda bb, i: (bb, i, 0)),
            _mod_spec(layer, 1, 2),
            _mod_spec(layer, 0, 2),
            pl.BlockSpec((d, n), lambda bb, i: (0, 0)),
        ],
        out_specs=out_specs,
        compiler_params=_cparams(("arbitrary", "arbitrary")),
        name=f"mod_proj_l{layer}",
    )(x, mod, mod, w)


def _diff_attn_kernel(slope_ref, lam_ref, g_ref, q_ref, k_ref, v_ref, o_ref,
                      rel_ref, m_ref, l_ref, acc_ref, *, tile, lam_init):
    h = pl.program_id(1)
    qi = pl.program_id(2)
    slope = slope_ref[h]
    half = DIFF_HEAD_DIM

    row = lax.broadcasted_iota(jnp.int32, (tile, tile), 0)
    col = lax.broadcasted_iota(jnp.int32, (tile, tile), 1)
    rel_ref[...] = (row - col).astype(F32) * slope

    lane = lax.broadcasted_iota(jnp.int32, (tile, LANES), 1)
    q = q_ref[...]
    qs = (q.astype(F32) * (DIFF_HEAD_DIM ** -0.5)).astype(BF16)
    zero = jnp.zeros_like(qs)
    qc = (jnp.where(lane < half, qs, zero), jnp.where(lane >= half, qs, zero))

    m_ref[...] = jnp.full(m_ref.shape, M_FLOOR, F32)
    l_ref[...] = jnp.zeros(l_ref.shape, F32)
    acc_ref[...] = jnp.zeros(acc_ref.shape, F32)

    def step(j, masked):
        k0 = pl.multiple_of(j * tile, tile)
        k = k_ref[pl.ds(k0, tile), :]
        v = v_ref[pl.ds(k0, tile), :]
        off = slope * ((qi - j) * tile).astype(F32)
        for c in range(2):
            s = _dot_nt(qc[c], k) - rel_ref[...]
            if masked:
                s = jnp.where(row >= col, s, NEG_INF)
            m_prev = m_ref[c]
            m_cur = jnp.max(s, axis=1, keepdims=True) - off
            m_next = jnp.maximum(m_prev, m_cur)
            alpha = jnp.exp(m_prev - m_next)
            shift = m_next + off
            p = jnp.concatenate(
                [jnp.exp(s[:, i * LANES:(i + 1) * LANES] - shift) for i in range(tile // LANES)], axis=1)
            l_ref[c] = alpha * l_ref[c] + jnp.sum(p, axis=1, keepdims=True)
            acc_ref[c] = alpha * acc_ref[c] + _dot(p.astype(BF16), v)
            m_ref[c] = m_next

    def body(j, carry):
        step(j, False)
        return carry

    lax.fori_loop(0, qi, body, 0)
    step(qi, True)

    lam_v = lam_ref[...]
    lam = (jnp.exp(jnp.sum(lam_v[0:1] * lam_v[1:2], axis=1, keepdims=True))
           - jnp.exp(jnp.sum(lam_v[2:3] * lam_v[3:4], axis=1, keepdims=True)) + lam_init)
    o = acc_ref[0] / l_ref[0] - lam * (acc_ref[1] / l_ref[1])
    o = o * lax.rsqrt(jnp.mean(o * o, axis=1, keepdims=True) + RMS_EPS)
    o_ref[...] = (o * g_ref[...] * (1.0 - lam_init)).astype(o_ref.dtype)


def _diff_attention(qkv, lam_vecs, subln_g, layer_idx):
    b, t, _ = qkv.shape
    hh = DIFF_HEADS
    tile = min(DIFF_TILE, t)
    lam_init = 0.8 - 0.6 * math.exp(-0.3 * layer_idx)
    slopes = jnp.asarray(_alibi_slopes(hh))
    return pl.pallas_call(
        functools.partial(_diff_attn_kernel, tile=tile, lam_init=lam_init),
        out_shape=jax.ShapeDtypeStruct((b, t, hh * DIFF_V_DIM), BF16),
        grid=(b, hh, t // tile),
        in_specs=[
            pl.BlockSpec(memory_space=pltpu.SMEM),
            pl.BlockSpec((4, DIFF_HEAD_DIM), lambda bb, h, i: (0, 0)),
            pl.BlockSpec((1, DIFF_V_DIM), lambda bb, h, i: (0, 0)),
            pl.BlockSpec((None, tile, LANES), lambda bb, h, i: (bb, i, h)),
            pl.BlockSpec((None, t, LANES), lambda bb, h, i: (bb, 0, hh + h)),
            pl.BlockSpec((None, t, LANES), lambda bb, h, i: (bb, 0, 2 * hh + h)),
        ],
        out_specs=pl.BlockSpec((None, tile, LANES), lambda bb, h, i: (bb, i, h)),
        scratch_shapes=[
            pltpu.VMEM((tile, tile), F32),
            pltpu.VMEM((2, tile, LANES), F32),
            pltpu.VMEM((2, tile, LANES), F32),
            pltpu.VMEM((2, tile, LANES), F32),
        ],
        compiler_params=_cparams(("arbitrary", "arbitrary", "arbitrary")),
        name="diff_attention",
    )(slopes, lam_vecs, subln_g.reshape(1, DIFF_V_DIM), qkv, qkv, qkv)


def _outproj_ln_kernel(o_ref, w_ref, x_ref, gate_ref, g_ref, b_ref, out_ref):
    y = _dot(o_ref[...], w_ref[...])
    z = DEEPNORM_ALPHA * x_ref[...] + (1.0 + gate_ref[...]) * y
    out_ref[...] = _layer_norm(z, g_ref[...], b_ref[...])


def _outproj_ln(o, w, x, mod, layer, ln_g, ln_b):
    b, t, d = x.shape
    dh = o.shape[-1]
    tm = min(PROJ_ROWS, t)
    return pl.pallas_call(
        _outproj_ln_kernel,
        out_shape=jax.ShapeDtypeStruct((b, t, d), F32),
        grid=(b, t // tm),
        in_specs=[
            pl.BlockSpec((None, tm, dh), lambda bb, i: (bb, i, 0)),
            pl.BlockSpec((dh, d), lambda bb, i: (0, 0)),
            pl.BlockSpec((None, tm, d), lambda bb, i: (bb, i, 0)),
            _mod_spec(layer, 2, 2),
            pl.BlockSpec((1, d), lambda bb, i: (0, 0)),
            pl.BlockSpec((1, d), lambda bb, i: (0, 0)),
        ],
        out_specs=pl.BlockSpec((None, tm, d), lambda bb, i: (bb, i, 0)),
        compiler_params=_cparams(("arbitrary", "arbitrary")),
        name=f"outproj_ln_l{layer}",
    )(o, w, x, mod, ln_g.reshape(1, d), ln_b.reshape(1, d))


def _ffn_kernel(x_ref, sc_ref, sh_ref, gate_ref, wg_ref, wu_ref, wd_ref, g_ref, b_ref, out_ref,
                h_ref, acc_ref):
    k = pl.program_id(2)

    @pl.when(k == 0)
    def _():
        h_ref[...] = (x_ref[...] * (1.0 + sc_ref[...]) + sh_ref[...]).astype(BF16)
        acc_ref[...] = jnp.zeros(acc_ref.shape, F32)

    h = h_ref[...]
    gp = _dot(h, wg_ref[...])
    up = _dot(h, wu_ref[...])
    act = (gp * jax.nn.sigmoid(gp) * up).astype(BF16)
    acc_ref[...] += _dot(act, wd_ref[...])

    @pl.when(k == pl.num_programs(2) - 1)
    def _():
        z = DEEPNORM_ALPHA * x_ref[...] + (1.0 + gate_ref[...]) * acc_ref[...]
        out_ref[...] = _layer_norm(z, g_ref[...], b_ref[...])


def _ffn_ln(x, mod, layer, w_gu, w_down, ln_g, ln_b):
    b, t, d = x.shape
    fh = w_down.shape[0]
    th = min(FFN_CHUNK, fh)
    nh = fh // th
    tm = min(PROJ_ROWS, t)
    return pl.pallas_call(
        _ffn_kernel,
        out_shape=jax.ShapeDtypeStruct((b, t, d), F32),
        grid=(b, t // tm, nh),
        in_specs=[
            pl.BlockSpec((None, tm, d), lambda bb, i, k: (bb, i, 0)),
            _mod_spec(layer, 4, 3),
            _mod_spec(layer, 3, 3),
            _mod_spec(layer, 5, 3),
            pl.BlockSpec((d, th), lambda bb, i, k: (0, k)),
            pl.BlockSpec((d, th), lambda bb, i, k: (0, nh + k)),
            pl.BlockSpec((th, d), lambda bb, i, k: (k, 0)),
            pl.BlockSpec((1, d), lambda bb, i, k: (0, 0)),
            pl.BlockSpec((1, d), lambda bb, i, k: (0, 0)),
        ],
        out_specs=pl.BlockSpec((None, tm, d), lambda bb, i, k: (bb, i, 0)),
        scratch_shapes=[pltpu.VMEM((tm, d), BF16), pltpu.VMEM((tm, d), F32)],
        compiler_params=_cparams(("arbitrary", "arbitrary", "arbitrary")),
        name=f"ffn_ln_l{layer}",
    )(x, mod, mod, mod, w_gu, w_gu, w_down, ln_g.reshape(1, d), ln_b.reshape(1, d))


def _gelu_tanh(x):
    return 0.5 * x * (1.0 + jnp.tanh(math.sqrt(2.0 / math.pi) * (x + 0.044715 * (x * x * x))))


def _compress_kernel(r_ref, pos_ref, w1_ref, w2_ref, o_ref):
    half = r_ref.shape[-1]
    n_rows = r_ref.shape[-2]
    out = None
    for kv in range(2):
        r = r_ref[kv]
        r_next = pltpu.roll(r, n_rows - 1, 0)
        a = (r + pos_ref[kv, :, :half]).astype(BF16)
        bnx = (r_next + pos_ref[kv, :, half:]).astype(BF16)
        hid = _dot(a, w1_ref[kv, :half, :]) + _dot(bnx, w1_ref[kv, half:, :])
        y = _dot(_gelu_tanh(hid).astype(BF16), w2_ref[kv])
        out = y if out is None else out + y
    o_ref[...] = out.astype(o_ref.dtype)


def _compress(r, pos, w1, w2pad):
    b, g, _, rows, width = r.shape
    return pl.pallas_call(
        _compress_kernel,
        out_shape=jax.ShapeDtypeStruct((b, g, rows, LANES), BF16),
        grid=(b, g),
        in_specs=[
            pl.BlockSpec((None, None, 2, rows, width), lambda bb, gg: (bb, gg, 0, 0, 0)),
            pl.BlockSpec((2, 1, 2 * width), lambda bb, gg: (0, 0, 0)),
            pl.BlockSpec((2, 2 * width, NSA_CMP_HIDDEN), lambda bb, gg: (0, 0, 0)),
            pl.BlockSpec((2, NSA_CMP_HIDDEN, LANES), lambda bb, gg: (0, 0, 0)),
        ],
        out_specs=pl.BlockSpec((None, None, rows, LANES), lambda bb, gg: (bb, gg, 0, 0)),
        compiler_params=_cparams(("arbitrary", "arbitrary")),
        name="nsa_compress",
    )(r, pos, w1, w2pad)


def _head_query(q_pair, head_in_pair, lane):
    d = NSA_HEAD_DIM
    qf = q_pair.astype(F32) * (d ** -0.5)
    if head_in_pair == 1:
        qf = pltpu.roll(qf, d, 1)
    return jnp.where(lane < d, qf, 0.0).astype(BF16)


def _merge_head_pair(o_even, o_odd, lane):
    return jnp.where(lane < NSA_HEAD_DIM, pltpu.roll(o_even, NSA_HEAD_DIM, 1), o_odd)


def _nsa_cmp_topk_kernel(slope_ref, q_ref, kvc_ref, gates_ref, ovt_ref,
                         oc_ref, sel_ref, any_ref, *, tq, n_rows, n_slc, k_sel, sub_tile):
    i = pl.program_id(1)
    q0 = i * tq
    row = lax.broadcasted_iota(jnp.int32, (tq, n_rows), 0)
    col = lax.broadcasted_iota(jnp.int32, (tq, n_rows), 1)
    dist = (q0 + row - (col * NSA_CMP_STRIDE + (NSA_CMP_LEN - 1))).astype(F32)
    valid = dist >= 0.0
    lane = lax.broadcasted_iota(jnp.int32, (tq, LANES), 1)

    blk_i = lax.broadcasted_iota(jnp.int32, (n_slc, tq), 0)
    qpos = q0 + lax.broadcasted_iota(jnp.int32, (n_slc, tq), 1)
    q_blk = lax.shift_right_logical(qpos, int(math.log2(NSA_SLC_LEN)))
    forced = (blk_i == 0) | (blk_i == q_blk) | (blk_i == q_blk - 1)
    causal_blk = blk_i * NSA_SLC_LEN <= qpos
    blk_f = blk_i.astype(F32)

    for g in range(NSA_GROUPS):
        kv = kvc_ref[g]
        imp_t = jnp.zeros((n_slc, tq), F32)
        outs = []
        for hh in range(NSA_HPG):
            h = g * NSA_HPG + hh
            qh = _head_query(q_ref[:, (h // 2) * LANES:(h // 2 + 1) * LANES], h % 2, lane)
            s = _dot_nt(qh, kv) - slope_ref[h] * dist
            s = jnp.where(valid, s, NEG_INF)
            m = jnp.max(s, axis=1, keepdims=True)
            p = jnp.where(valid, jnp.exp(s - m), 0.0)
            l = jnp.sum(p, axis=1, keepdims=True)
            inv = jnp.where(l > 0.0, 1.0 / l, 0.0)
            pb = (p * inv).astype(BF16)
            o = _dot(pb, kv)
            imp_t = imp_t + _dot_nt(ovt_ref[...], pb)
            outs.append(o * gates_ref[:, g * LANES + 3 * hh:g * LANES + 3 * hh + 1])
        for pr in range(NSA_HPG // 2):
            c0 = (g * NSA_HPG // 2 + pr) * LANES
            oc_ref[:, c0:c0 + LANES] = _merge_head_pair(outs[2 * pr], outs[2 * pr + 1], lane)

        imp = jnp.where(causal_blk, jnp.where(forced, FORCED_SCORE, imp_t), NEG_INF)
        sel = jnp.zeros((n_slc, tq), F32)
        for _ in range(k_sel):
            mx = jnp.max(imp, axis=0, keepdims=True)
            cand = jnp.where(imp == mx, blk_f, float(n_slc))
            pick = blk_f == jnp.min(cand, axis=0, keepdims=True)
            sel = jnp.where(pick, 1.0, sel)
            imp = jnp.where(pick, -3e38, imp)
        sel = jnp.where(causal_blk, sel, 0.0)
        sel_q = sel.T
        sel_ref[g] = sel_q.astype(sel_ref.dtype)
        for sub in range(tq // sub_tile):
            r = sub * NSA_GROUPS + g
            any_ref[r:r + 1, :] = jnp.max(sel_q[sub * sub_tile:(sub + 1) * sub_tile], axis=0, keepdims=True)


def _nsa_cmp_topk(qkv, kvc, gates, ovt, tq, sub_tile):
    b, t, _ = qkv.shape
    g = NSA_GROUPS
    n_rows = kvc.shape[2]
    n_slc = t // NSA_SLC_LEN
    k_sel = min(NSA_SLC_TOPK, n_slc)
    nq = t // tq
    assert n_slc == LANES and (tq // sub_tile) * g == 8
    slopes = jnp.asarray(_alibi_slopes(NSA_HEADS))
    return pl.pallas_call(
        functools.partial(_nsa_cmp_topk_kernel, tq=tq, n_rows=n_rows, n_slc=n_slc, k_sel=k_sel,
                          sub_tile=sub_tile),
        out_shape=[
            jax.ShapeDtypeStruct((b, t, NSA_HEADS * NSA_HEAD_DIM), F32),
            jax.ShapeDtypeStruct((b, g, t, n_slc), BF16),
            jax.ShapeDtypeStruct((b, nq, 8, n_slc), F32),
        ],
        grid=(b, nq),
        in_specs=[
            pl.BlockSpec(memory_space=pltpu.SMEM),
            pl.BlockSpec((None, tq, NSA_HEADS * NSA_HEAD_DIM), lambda bb, i: (bb, i, 0)),
            pl.BlockSpec((None, g, n_rows, LANES), lambda bb, i: (bb, 0, 0, 0)),
            pl.BlockSpec((None, tq, g * LANES), lambda bb, i: (bb, i, 0)),
            pl.BlockSpec((n_slc, n_rows), lambda bb, i: (0, 0)),
        ],
        out_specs=[
            pl.BlockSpec((None, tq, NSA_HEADS * NSA_HEAD_DIM), lambda bb, i: (bb, i, 0)),
            pl.BlockSpec((None, g, tq, n_slc), lambda bb, i: (bb, 0, i, 0)),
            pl.BlockSpec((None, None, 8, n_slc), lambda bb, i: (bb, i, 0, 0)),
        ],
        compiler_params=_cparams(("arbitrary", "arbitrary")),
        name="nsa_cmp_topk",
    )(slopes, qkv, kvc, gates, ovt)


def _nsa_sel_win_kernel(bits_ref, slope_ref, q_ref, kvs_ref, kvw_ref, sel_ref, oc_ref, gates_ref, esel_ref,
                        o_ref, m_ref, l_ref, acc_ref, *, tq, tk, win_span):
    b = pl.program_id(0)
    g = pl.program_id(1)
    i = pl.program_id(2)
    q0 = i * tq
    hpg = NSA_HPG
    lane = lax.broadcasted_iota(jnp.int32, (tq, LANES), 1)
    slopes = [slope_ref[g * hpg + hh] for hh in range(hpg)]

    q4 = jnp.concatenate(
        [_head_query(q_ref[:, (hh // 2) * LANES:(hh // 2 + 1) * LANES], hh % 2, lane) for hh in range(hpg)],
        axis=0)

    m_ref[...] = jnp.full(m_ref.shape, M_FLOOR, F32)
    l_ref[...] = jnp.zeros(l_ref.shape, F32)
    acc_ref[...] = jnp.zeros(acc_ref.shape, F32)
    rel = (lax.broadcasted_iota(jnp.int32, (tq, tk), 0)
           - lax.broadcasted_iota(jnp.int32, (tq, tk), 1)).astype(F32)
    sel = sel_ref[...]
    bits = bits_ref[(b * pl.num_programs(1) + g) * pl.num_programs(2) + i]

    def chunk(j, carry):
        @pl.when((lax.shift_right_logical(bits, j) & 1) == 1)
        def _():
            k0 = pl.multiple_of(j * tk, tk)
            kv = kvs_ref[pl.ds(k0, tk), :]
            s_all = _dot_nt(q4, kv)
            picked = _dot(sel, esel_ref[j])
            dist = rel + (q0 - k0).astype(F32)
            madd = jnp.where((picked > 0.5) & (dist >= 0.0), 0.0, NEG_INF)
            for hh in range(hpg):
                r0 = hh * tq
                s = s_all[r0:r0 + tq] + (madd - slopes[hh] * dist)
                m_prev = m_ref[r0:r0 + tq]
                m_next = jnp.maximum(m_prev, jnp.max(s, axis=1, keepdims=True))
                alpha = jnp.exp(m_prev - m_next)
                p = jnp.concatenate(
                    [jnp.exp(s[:, c * LANES:(c + 1) * LANES] - m_next) for c in range(tk // LANES)], axis=1)
                l_ref[r0:r0 + tq] = alpha * l_ref[r0:r0 + tq] + jnp.sum(p, axis=1, keepdims=True)
                acc_ref[r0:r0 + tq] = alpha * acc_ref[r0:r0 + tq] + _dot(p.astype(BF16), kv)
                m_ref[r0:r0 + tq] = m_next
        return carry

    n_chunks = (q0 + tq + tk - 1) // tk
    lax.fori_loop(0, n_chunks, chunk, 0)

    start = pl.multiple_of(jnp.maximum(q0 + tq - win_span, 0), tq)
    kvw = kvw_ref[pl.ds(start, win_span), :]
    sw_all = _dot_nt(q4, kvw)
    dist_w = (lax.broadcasted_iota(jnp.int32, (tq, win_span), 0)
              - lax.broadcasted_iota(jnp.int32, (tq, win_span), 1) + (q0 - start)).astype(F32)
    valid_w = (dist_w >= 0.0) & (dist_w < float(NSA_WINDOW))

    gates = gates_ref[...]
    outs = []
    for hh in range(hpg):
        r0 = hh * tq
        l_s = l_ref[r0:r0 + tq]
        o_s = acc_ref[r0:r0 + tq] * jnp.where(l_s > 0.0, 1.0 / l_s, 0.0)
        s = jnp.where(valid_w, sw_all[r0:r0 + tq] - slopes[hh] * dist_w, NEG_INF)
        m = jnp.max(s, axis=1, keepdims=True)
        p = jnp.exp(s - m)
        pw = (p * (1.0 / jnp.sum(p, axis=1, keepdims=True))).astype(BF16)
        o_w = _dot(pw, kvw)
        outs.append(gates[:, 3 * hh + 1:3 * hh + 2] * o_s + gates[:, 3 * hh + 2:3 * hh + 3] * o_w)
    for pr in range(hpg // 2):
        merged = _merge_head_pair(outs[2 * pr], outs[2 * pr + 1], lane)
        o_ref[:, pr * LANES:(pr + 1) * LANES] = (
            merged + oc_ref[:, pr * LANES:(pr + 1) * LANES]).astype(o_ref.dtype)


def _nsa_sel_win(bits, qkv, sel, oc, gates, esel, tq, tk):
    b, t, _ = qkv.shape
    g = NSA_GROUPS
    n_slc = t // NSA_SLC_LEN
    nq = t // tq
    hd = NSA_HPG * NSA_HEAD_DIM
    kv_col0 = NSA_HEADS * NSA_HEAD_DIM // LANES
    win_span = min(NSA_WINDOW + tq, t)
    slopes = jnp.asarray(_alibi_slopes(NSA_HEADS))
    grid_spec = pltpu.PrefetchScalarGridSpec(
        num_scalar_prefetch=1,
        grid=(b, g, nq),
        in_specs=[
            pl.BlockSpec(memory_space=pltpu.SMEM),
            pl.BlockSpec((None, tq, hd), lambda bb, gg, i, bits: (bb, i, gg)),
            pl.BlockSpec((None, t, LANES), lambda bb, gg, i, bits: (bb, 0, kv_col0 + gg)),
            pl.BlockSpec((None, t, LANES), lambda bb, gg, i, bits: (bb, 0, kv_col0 + g + gg)),
            pl.BlockSpec((None, None, tq, n_slc), lambda bb, gg, i, bits: (bb, gg, i, 0)),
            pl.BlockSpec((None, tq, hd), lambda bb, gg, i, bits: (bb, i, gg)),
            pl.BlockSpec((None, tq, LANES), lambda bb, gg, i, bits: (bb, i, gg)),
            pl.BlockSpec(esel.shape, lambda bb, gg, i, bits: (0, 0, 0)),
        ],
        out_specs=pl.BlockSpec((None, tq, hd), lambda bb, gg, i, bits: (bb, i, gg)),
        scratch_shapes=[
            pltpu.VMEM((NSA_HPG * tq, LANES), F32),
            pltpu.VMEM((NSA_HPG * tq, LANES), F32),
            pltpu.VMEM((NSA_HPG * tq, LANES), F32),
        ],
    )
    return pl.pallas_call(
        functools.partial(_nsa_sel_win_kernel, tq=tq, tk=tk, win_span=win_span),
        out_shape=jax.ShapeDtypeStruct((b, t, NSA_HEADS * NSA_HEAD_DIM), BF16),
        grid_spec=grid_spec,
        compiler_params=_cparams(("arbitrary", "arbitrary", "arbitrary")),
        name="nsa_sel_win",
    )(bits, slopes, qkv, qkv, qkv, sel, oc, gates, esel)


def _nsa_weight_layout():
    hd, g, d = NSA_HEADS * NSA_HEAD_DIM, NSA_GROUPS, NSA_HEAD_DIM
    base = {name: hd + n * g * d for n, name in enumerate(["kc", "vc", "ks", "vs", "kw", "vw"])}
    gate0 = hd + 6 * g * d
    cols = list(range(hd))
    for kn, vn in (("ks", "vs"), ("kw", "vw"), ("kc", "vc")):
        for gg in range(g):
            cols += list(range(base[kn] + gg * d, base[kn] + (gg + 1) * d))
            cols += list(range(base[vn] + gg * d, base[vn] + (gg + 1) * d))
    for gg in range(g):
        n_gate = NSA_HPG * 3
        cols += list(range(gate0 + gg * n_gate, gate0 + (gg + 1) * n_gate)) + [-1] * (LANES - n_gate)
    return np.asarray(cols, np.int32)


def _permute_columns(w, cols):
    picked = jnp.take(w, jnp.asarray(np.maximum(cols, 0)), axis=1)
    return jnp.where(jnp.asarray(cols >= 0)[None, :], picked, 0.0)


def _overlap_t(n_rows, n_slc):
    n_cmp = n_rows - 1
    c_start = np.arange(n_rows) * NSA_CMP_STRIDE
    s_start = np.arange(n_slc) * NSA_SLC_LEN
    ov = (np.minimum(c_start[:, None] + NSA_CMP_LEN, s_start[None, :] + NSA_SLC_LEN)
          - np.maximum(c_start[:, None], s_start[None, :]))
    ov = np.clip(ov, 0, None).astype(np.float32) / NSA_CMP_LEN
    ov[n_cmp:] = 0.0
    return jnp.asarray(ov.T, BF16)


def _block_expander(n_slc, tk):
    n_chunks = n_slc * NSA_SLC_LEN // tk
    key_blk = (np.arange(n_chunks)[:, None] * tk + np.arange(tk)[None, :]) // NSA_SLC_LEN
    e = (np.arange(n_slc)[None, :, None] == key_blk[:, None, :]).astype(np.float32)
    return jnp.asarray(e, BF16)


def _chunk_bits(any_sel, sub_per_tile, blocks_per_chunk):
    b, nq, _, n_slc = any_sel.shape
    g = NSA_GROUPS
    n_chunks = n_slc // blocks_per_chunk
    a = any_sel.reshape(b, nq, sub_per_tile, g, n_chunks, blocks_per_chunk).max(axis=-1) > 0.5
    a = a.transpose(0, 3, 1, 2, 4).reshape(b * g * nq * sub_per_tile, n_chunks)
    weights = jnp.left_shift(jnp.uint32(1), jnp.arange(n_chunks, dtype=jnp.uint32))
    return lax.bitcast_convert_type(jnp.sum(jnp.where(a, weights[None, :], jnp.uint32(0)), axis=1,
                                            dtype=jnp.uint32), jnp.int32)


def _diff_mixer(x, mod, layer, w_in, w_out, lam_q1, lam_k1, lam_q2, lam_k2, subln_g):
    n = w_in.shape[1]
    (qkv,) = _modulated_projection(x, mod, layer, w_in.astype(BF16), [(0, n, "id")], [BF16])
    lam_vecs = jnp.stack([lam_q1, lam_k1, lam_q2, lam_k2]).astype(F32)
    o = _diff_attention(qkv, lam_vecs, subln_g.astype(F32), layer)
    return o, w_out.astype(BF16)


def _nsa_mixer(x, mod, layer, w_in, pos_k, pos_v, k_w1, k_w2, v_w1, v_w2, w_out):
    b, t, _ = x.shape
    g, d = NSA_GROUPS, NSA_HEAD_DIM
    hd = NSA_HEADS * d
    cols = _nsa_weight_layout()
    w = _permute_columns(w_in, cols).astype(BF16)
    n_main = hd + 2 * (2 * g * d)
    n_cmp_cols = 2 * g * d
    plan = [(0, n_main, "id"), (n_main, n_main + n_cmp_cols, "id"),
            (n_main + n_cmp_cols, n_main + n_cmp_cols + g * LANES, "sigmoid")]
    qkv, kvc_tok, gates = _modulated_projection(x, mod, layer, w, plan, [BF16, F32, F32])

    rows = t // NSA_CMP_STRIDE
    r = kvc_tok.reshape(b, rows, NSA_CMP_STRIDE, g, 2, d).transpose(0, 3, 4, 1, 2, 5)
    r = r.reshape(b, g, 2, rows, NSA_CMP_STRIDE * d)
    pos = jnp.stack([pos_k.reshape(1, -1), pos_v.reshape(1, -1)]).astype(F32)
    w1 = jnp.stack([k_w1, v_w1]).astype(BF16)
    zeros = jnp.zeros((NSA_CMP_HIDDEN, d), F32)
    w2pad = jnp.stack([jnp.concatenate([k_w2, zeros], axis=1),
                       jnp.concatenate([zeros, v_w2], axis=1)]).astype(BF16)
    kvc = _compress(r, pos, w1, w2pad)

    n_slc = t // NSA_SLC_LEN
    tq_sel = min(SEL_Q_TILE, t)
    tq_cmp = min(CMP_Q_TILE, t)
    tk = min(SEL_K_TILE, t)
    oc, sel, any_sel = _nsa_cmp_topk(qkv, kvc, gates, _overlap_t(rows, n_slc), tq_cmp, tq_sel)
    bits = _chunk_bits(any_sel, tq_cmp // tq_sel, tk // NSA_SLC_LEN)
    o = _nsa_sel_win(bits, qkv, sel, oc, gates, _block_expander(n_slc, tk), tq_sel, tk)
    return o, w_out.astype(BF16)


def kernel(x, c, diff_w_in, diff_w_out, diff_lam_q1, diff_lam_k1, diff_lam_q2, diff_lam_k2, diff_subln_g,
           nsa_w_in, nsa_cmp_pos_k, nsa_cmp_pos_v, nsa_cmp_k_w1, nsa_cmp_k_w2, nsa_cmp_v_w1, nsa_cmp_v_w2,
           nsa_w_out, ada_w, ada_b, ln_mix_g, ln_mix_b, ln_ffn_g, ln_ffn_b, ffn_w_gu, ffn_w_down):
    mod = _ada_modulation(c, ada_w, ada_b)
    for i in range(DEPTH):
        j = i // N_MIXERS
        if i % N_MIXERS == 0:
            o, w_out = _diff_mixer(x, mod, i, diff_w_in[j], diff_w_out[j], diff_lam_q1[j], diff_lam_k1[j],
                                   diff_lam_q2[j], diff_lam_k2[j], diff_subln_g[j])
        else:
            o, w_out = _nsa_mixer(x, mod, i, nsa_w_in[j], nsa_cmp_pos_k[j], nsa_cmp_pos_v[j],
                                  nsa_cmp_k_w1[j], nsa_cmp_k_w2[j], nsa_cmp_v_w1[j], nsa_cmp_v_w2[j],
                                  nsa_w_out[j])
        x = _outproj_ln(o, w_out, x, mod, i, ln_mix_g[i], ln_mix_b[i])
        x = _ffn_ln(x, mod, i, ffn_w_gu[i].astype(BF16), ffn_w_down[i].astype(BF16), ln_ffn_g[i], ln_ffn_b[i])
    return x
```

```python
import functools
import math

import numpy as np
import jax
import jax.numpy as jnp
from jax import lax
from jax.experimental import pallas as pl
from jax.experimental.pallas import tpu as pltpu

F32 = jnp.float32
BF16 = jnp.bfloat16

D_MODEL = 1024
DEPTH = 2
N_MIXERS = 2
DEEPNORM_ALPHA = (2 * DEPTH) ** 0.25
LN_EPS = 1e-5
RMS_EPS = 1e-5
NEG_INF = -1e30
FORCED_SCORE = 1e6

DIFF_HEADS = 8
DIFF_HEAD_DIM = 64
DIFF_V_DIM = 2 * DIFF_HEAD_DIM

NSA_HEADS = 16
NSA_GROUPS = 4
NSA_HEAD_DIM = 64
NSA_HPG = NSA_HEADS // NSA_GROUPS
NSA_CMP_LEN = 32
NSA_CMP_STRIDE = 16
NSA_CMP_HIDDEN = 128
NSA_SLC_LEN = 64
NSA_SLC_TOPK = 16
NSA_WINDOW = 512

FFN_HIDDEN = -(-8 * D_MODEL // (3 * 256)) * 256

LANES = 128
VMEM_LIMIT = 56 * 1024 * 1024
PROJ_ROWS = 512
FFN_CHUNK = FFN_HIDDEN // 2
DIFF_TILE = 256
CMP_Q_TILE = 256
SEL_Q_TILE = 128
SEL_K_TILE = 256
M_FLOOR = -1e29


def _cparams(sem):
    return pltpu.CompilerParams(dimension_semantics=sem, vmem_limit_bytes=VMEM_LIMIT)


def _dot(a, b):
    return jnp.dot(a, b, preferred_element_type=F32)


def _dot_nt(a, b):
    return lax.dot_general(a, b, (((1,), (1,)), ((), ())), preferred_element_type=F32)


def _layer_norm(z, g, b):
    mu = jnp.mean(z, axis=-1, keepdims=True)
    zc = z - mu
    var = jnp.mean(zc * zc, axis=-1, keepdims=True)
    return zc * lax.rsqrt(var + LN_EPS) * g + b


def _alibi_slopes(n_heads):
    return (2.0 ** (-8.0 * np.arange(1, n_heads + 1, dtype=np.float32) / n_heads)).astype(np.float32)


def _ada_kernel(c_ref, w_ref, b_ref, o_ref):
    c = c_ref[...]
    s = (c * jax.nn.sigmoid(c)).astype(BF16)
    o_ref[0] = _dot(s, w_ref[0].astype(BF16)) + b_ref[0]


def _ada_modulation(c, ada_w, ada_b):
    depth, d, d6 = ada_w.shape
    b = c.shape[0]
    rows = 8
    c_pad = jnp.zeros((rows, d), F32).at[:b].set(c)
    out = pl.pallas_call(
        _ada_kernel,
        out_shape=jax.ShapeDtypeStruct((depth, rows, d6), F32),
        grid=(depth, d6 // d),
        in_specs=[
            pl.BlockSpec((rows, d), lambda l, j: (0, 0)),
            pl.BlockSpec((1, d, d), lambda l, j: (l, 0, j)),
            pl.BlockSpec((1, 1, d), lambda l, j: (l, 0, j)),
        ],
        out_specs=pl.BlockSpec((1, rows, d), lambda l, j: (l, 0, j)),
        compiler_params=_cparams(("arbitrary", "arbitrary")),
        name="ada_modulation",
    )(c_pad, ada_w, ada_b.reshape(depth, 1, d6))
    return out[:, :b].reshape(depth, b, 6, 1, d)


def _mod_spec(layer, which, n_grid):
    d = D_MODEL
    if n_grid == 2:
        return pl.BlockSpec((None, None, None, 1, d), lambda b, i: (layer, b, which, 0, 0))
    return pl.BlockSpec((None, None, None, 1, d), lambda b, i, k: (layer, b, which, 0, 0))


def _proj_kernel(x_ref, sc_ref, sh_ref, w_ref, wt_ref, *out_refs, plan, t_plan):
    h = (x_ref[...] * (1.0 + sc_ref[...]) + sh_ref[...]).astype(BF16)
    for ref, (c0, c1, kind), in zip(out_refs, plan):
        step = 1024
        for s0 in range(c0, c1, step):
            s1 = min(s0 + step, c1)
            y = _dot(h, w_ref[:, s0:s1])
            if kind == "sigmoid":
                y = jax.nn.sigmoid(y)
            ref[:, s0 - c0:s1 - c0] = y.astype(ref.dtype)
    for ref, (r0, r1) in zip(out_refs[len(plan):], t_plan):
        ref[...] = _dot_nt(wt_ref[r0:r1, :], h).astype(ref.dtype)


def _modulated_projection(x, mod, layer, w, plan, out_dtypes, wt, t_plan):
    b, t, d = x.shape
    n = w.shape[1]
    tm = min(PROJ_ROWS, t)
    out_shape = [jax.ShapeDtypeStruct((b, t, c1 - c0), dt) for (c0, c1, _), dt in zip(plan, out_dtypes)]
    out_specs = [pl.BlockSpec((None, tm, c1 - c0), lambda bb, i: (bb, i, 0)) for (c0, c1, _) in plan]
    out_shape += [jax.ShapeDtypeStruct((b, r1 - r0, t), BF16) for (r0, r1) in t_plan]
    out_specs += [pl.BlockSpec((None, r1 - r0, tm), lambda bb, i: (bb, 0, i)) for (r0, r1) in t_plan]
    return pl.pallas_call(
        functools.partial(_proj_kernel, plan=plan, t_plan=t_plan),
        out_shape=out_shape,
        grid=(b, t // tm),
        in_specs=[
            pl.BlockSpec((None, tm, d), lambda bb, i: (bb, i, 0)),
            _mod_spec(layer, 1, 2),
            _mod_spec(layer, 0, 2),
            pl.BlockSpec((d, n), lambda bb, i: (0, 0)),
            pl.BlockSpec(wt.shape, lambda bb, i: (0, 0)),
        ],
        out_specs=out_specs,
        compiler_params=_cparams(("arbitrary", "arbitrary")),
        name=f"mod_proj_l{layer}",
    )(x, mod, mod, w, wt)


def _diff_attn_kernel(slope_ref, lam_ref, g_ref, q_ref, k_ref, vt_ref, o_ref,
                      rel_ref, s_ref, mc_ref, st_ref, acc_ref, *, tile, lam_init):
    h = pl.program_id(1)
    qi = pl.program_id(2)
    slope = slope_ref[h]
    half = DIFF_HEAD_DIM

    key = lax.broadcasted_iota(jnp.int32, (tile, tile), 0)
    qry = lax.broadcasted_iota(jnp.int32, (tile, tile), 1)
    rel_ref[...] = (qry - key).astype(F32) * slope

    lane = lax.broadcasted_iota(jnp.int32, (tile, LANES), 1)
    qs = (q_ref[...].astype(F32) * (DIFF_HEAD_DIM ** -0.5)).astype(BF16)
    zero = jnp.zeros_like(qs)
    qc = (jnp.where(lane < half, qs, zero), jnp.where(lane >= half, qs, zero))

    acc_ref[...] = jnp.zeros(acc_ref.shape, F32)
    for c in range(2):
        st_ref[2 * c:2 * c + 1, :] = jnp.full((1, tile), M_FLOOR, F32)
        st_ref[2 * c + 1:2 * c + 2, :] = jnp.zeros((1, tile), F32)

    def scores(j):
        k = k_ref[pl.ds(pl.multiple_of(j * tile, tile), tile), :]
        return [_dot_nt(k, qc[c]) for c in range(2)]

    def produce(j, raw, diagonal):
        slot = j & 1
        for c in range(2):
            s = raw[c] - rel_ref[...]
            if diagonal is True:
                s = jnp.where(key <= qry, s, NEG_INF)
            elif diagonal is not False:
                s = jnp.where((key <= qry) | jnp.logical_not(diagonal), s, NEG_INF)
            s_ref[slot, c] = s
            mc_ref[slot, c:c + 1, :] = jnp.max(s, axis=0, keepdims=True)

    def consume(j):
        slot = j & 1
        vt = vt_ref[:, pl.ds(pl.multiple_of(j * tile, tile), tile)]
        off = slope * ((qi - j) * tile).astype(F32)
        for c in range(2):
            m_prev = st_ref[2 * c:2 * c + 1, :]
            l_prev = st_ref[2 * c + 1:2 * c + 2, :]
            m_next = jnp.maximum(m_prev, mc_ref[slot, c:c + 1, :] - off)
            alpha = jnp.exp(m_prev - m_next)
            p = jnp.exp(s_ref[slot, c] - (m_next + off))
            st_ref[2 * c:2 * c + 1, :] = m_next
            st_ref[2 * c + 1:2 * c + 2, :] = alpha * l_prev + jnp.sum(p, axis=0, keepdims=True)
            acc_ref[c] = alpha * acc_ref[c] + _dot(vt, p.astype(BF16))

    produce(0, scores(0), qi == 0)

    def body(t, carry):
        raw = scores(t)
        consume(t - 1)
        produce(t, raw, False)
        return carry

    lax.fori_loop(1, qi, body, 0)

    @pl.when(qi > 0)
    def _():
        raw = scores(qi)
        consume(qi - 1)
        produce(qi, raw, True)

    consume(qi)

    l0 = st_ref[1:2, :]
    l1 = st_ref[3:4, :]
    lam_v = lam_ref[...]
    lam = (jnp.exp(jnp.sum(lam_v[0:1] * lam_v[1:2], axis=1, keepdims=True))
           - jnp.exp(jnp.sum(lam_v[2:3] * lam_v[3:4], axis=1, keepdims=True)) + lam_init)
    o = acc_ref[0] / l0 - lam * (acc_ref[1] / l1)
    o = o * lax.rsqrt(jnp.mean(o * o, axis=0, keepdims=True) + RMS_EPS)
    o_ref[...] = ((o * g_ref[...] * (1.0 - lam_init)).T).astype(o_ref.dtype)


def _diff_attention(qk, vt, lam_vecs, subln_g, layer_idx):
    b, t, _ = qk.shape
    hh = DIFF_HEADS
    tile = min(DIFF_TILE, t)
    lam_init = 0.8 - 0.6 * math.exp(-0.3 * layer_idx)
    slopes = jnp.asarray(_alibi_slopes(hh))
    return pl.pallas_call(
        functools.partial(_diff_attn_kernel, tile=tile, lam_init=lam_init),
        out_shape=jax.ShapeDtypeStruct((b, t, hh * DIFF_V_DIM), BF16),
        grid=(b, hh, t // tile),
        in_specs=[
            pl.BlockSpec(memory_space=pltpu.SMEM),
            pl.BlockSpec((4, DIFF_HEAD_DIM), lambda bb, h, i: (0, 0)),
            pl.BlockSpec((DIFF_V_DIM, 1), lambda bb, h, i: (0, 0)),
            pl.BlockSpec((None, tile, LANES), lambda bb, h, i: (bb, i, h)),
            pl.BlockSpec((None, t, LANES), lambda bb, h, i: (bb, 0, hh + h)),
            pl.BlockSpec((None, DIFF_V_DIM, t), lambda bb, h, i: (bb, h, 0)),
        ],
        out_specs=pl.BlockSpec((None, tile, LANES), lambda bb, h, i: (bb, i, h)),
        scratch_shapes=[
            pltpu.VMEM((tile, tile), F32),
            pltpu.VMEM((2, 2, tile, tile), F32),
            pltpu.VMEM((2, 8, tile), F32),
            pltpu.VMEM((8, tile), F32),
            pltpu.VMEM((2, DIFF_V_DIM, tile), F32),
        ],
        compiler_params=_cparams(("arbitrary", "arbitrary", "arbitrary")),
        name="diff_attention",
    )(slopes, lam_vecs, subln_g.reshape(DIFF_V_DIM, 1), qk, qk, vt)


def _outproj_ln_kernel(o_ref, w_ref, x_ref, gate_ref, g_ref, b_ref, out_ref):
    y = _dot(o_ref[...], w_ref[...])
    z = DEEPNORM_ALPHA * x_ref[...] + (1.0 + gate_ref[...]) * y
    out_ref[...] = _layer_norm(z, g_ref[...], b_ref[...])


def _outproj_ln(o, w, x, mod, layer, ln_g, ln_b):
    b, t, d = x.shape
    dh = o.shape[-1]
    tm = min(PROJ_ROWS, t)
    return pl.pallas_call(
        _outproj_ln_kernel,
        out_shape=jax.ShapeDtypeStruct((b, t, d), F32),
        grid=(b, t // tm),
        in_specs=[
            pl.BlockSpec((None, tm, dh), lambda bb, i: (bb, i, 0)),
            pl.BlockSpec((dh, d), lambda bb, i: (0, 0)),
            pl.BlockSpec((None, tm, d), lambda bb, i: (bb, i, 0)),
            _mod_spec(layer, 2, 2),
            pl.BlockSpec((1, d), lambda bb, i: (0, 0)),
            pl.BlockSpec((1, d), lambda bb, i: (0, 0)),
        ],
        out_specs=pl.BlockSpec((None, tm, d), lambda bb, i: (bb, i, 0)),
        compiler_params=_cparams(("arbitrary", "arbitrary")),
        name=f"outproj_ln_l{layer}",
    )(o, w, x, mod, ln_g.reshape(1, d), ln_b.reshape(1, d))


def _ffn_kernel(x_ref, sc_ref, sh_ref, gate_ref, wg_ref, wu_ref, wd_ref, g_ref, b_ref, out_ref,
                h_ref, acc_ref):
    k = pl.program_id(2)

    @pl.when(k == 0)
    def _():
        h_ref[...] = (x_ref[...] * (1.0 + sc_ref[...]) + sh_ref[...]).astype(BF16)
        acc_ref[...] = jnp.zeros(acc_ref.shape, F32)

    h = h_ref[...]
    gp = _dot(h, wg_ref[...])
    up = _dot(h, wu_ref[...])
    act = (gp * jax.nn.sigmoid(gp) * up).astype(BF16)
    acc_ref[...] += _dot(act, wd_ref[...])

    @pl.when(k == pl.num_programs(2) - 1)
    def _():
        z = DEEPNORM_ALPHA * x_ref[...] + (1.0 + gate_ref[...]) * acc_ref[...]
        out_ref[...] = _layer_norm(z, g_ref[...], b_ref[...])


def _ffn_ln(x, mod, layer, w_gu, w_down, ln_g, ln_b):
    b, t, d = x.shape
    fh = w_down.shape[0]
    th = min(FFN_CHUNK, fh)
    nh = fh // th
    tm = min(PROJ_ROWS, t)
    return pl.pallas_call(
        _ffn_kernel,
        out_shape=jax.ShapeDtypeStruct((b, t, d), F32),
        grid=(b, t // tm, nh),
        in_specs=[
            pl.BlockSpec((None, tm, d), lambda bb, i, k: (bb, i, 0)),
            _mod_spec(layer, 4, 3),
            _mod_spec(layer, 3, 3),
            _mod_spec(layer, 5, 3),
            pl.BlockSpec((d, th), lambda bb, i, k: (0, k)),
            pl.BlockSpec((d, th), lambda bb, i, k: (0, nh + k)),
            pl.BlockSpec((th, d), lambda bb, i, k: (k, 0)),
            pl.BlockSpec((1, d), lambda bb, i, k: (0, 0)),
            pl.BlockSpec((1, d), lambda bb, i, k: (0, 0)),
        ],
        out_specs=pl.BlockSpec((None, tm, d), lambda bb, i, k: (bb, i, 0)),
        scratch_shapes=[pltpu.VMEM((tm, d), BF16), pltpu.VMEM((tm, d), F32)],
        compiler_params=_cparams(("arbitrary", "arbitrary", "arbitrary")),
        name=f"ffn_ln_l{layer}",
    )(x, mod, mod, mod, w_gu, w_gu, w_down, ln_g.reshape(1, d), ln_b.reshape(1, d))


def _gelu_tanh(x):
    return 0.5 * x * (1.0 + jnp.tanh(math.sqrt(2.0 / math.pi) * (x + 0.044715 * (x * x * x))))


def _compress_kernel(r_ref, pos_ref, w1_ref, w2_ref, o_ref):
    half = r_ref.shape[-1]
    n_rows = r_ref.shape[-2]
    out = None
    for kv in range(2):
        r = r_ref[kv]
        r_next = pltpu.roll(r, n_rows - 1, 0)
        a = (r + pos_ref[kv, :, :half]).astype(BF16)
        bnx = (r_next + pos_ref[kv, :, half:]).astype(BF16)
        hid = _dot(a, w1_ref[kv, :half, :]) + _dot(bnx, w1_ref[kv, half:, :])
        y = _dot(_gelu_tanh(hid).astype(BF16), w2_ref[kv])
        out = y if out is None else out + y
    o_ref[...] = out.astype(o_ref.dtype)


def _compress(r, pos, w1, w2pad):
    b, g, _, rows, width = r.shape
    return pl.pallas_call(
        _compress_kernel,
        out_shape=jax.ShapeDtypeStruct((b, g, rows, LANES), BF16),
        grid=(b, g),
        in_specs=[
            pl.BlockSpec((None, None, 2, rows, width), lambda bb, gg: (bb, gg, 0, 0, 0)),
            pl.BlockSpec((2, 1, 2 * width), lambda bb, gg: (0, 0, 0)),
            pl.BlockSpec((2, 2 * width, NSA_CMP_HIDDEN), lambda bb, gg: (0, 0, 0)),
            pl.BlockSpec((2, NSA_CMP_HIDDEN, LANES), lambda bb, gg: (0, 0, 0)),
        ],
        out_specs=pl.BlockSpec((None, None, rows, LANES), lambda bb, gg: (bb, gg, 0, 0)),
        compiler_params=_cparams(("arbitrary", "arbitrary")),
        name="nsa_compress",
    )(r, pos, w1, w2pad)


def _head_query(q_pair, head_in_pair, lane):
    d = NSA_HEAD_DIM
    qf = q_pair.astype(F32) * (d ** -0.5)
    if head_in_pair == 1:
        qf = pltpu.roll(qf, d, 1)
    return jnp.where(lane < d, qf, 0.0).astype(BF16)


def _merge_head_pair(o_even, o_odd, lane):
    return jnp.where(lane < NSA_HEAD_DIM, pltpu.roll(o_even, NSA_HEAD_DIM, 1), o_odd)


def _nsa_cmp_topk_kernel(slope_ref, q_ref, kvc_ref, gates_ref, ovt_ref,
                         oc_ref, sel_ref, any_ref, *, tq, n_rows, n_slc, k_sel, sub_tile):
    i = pl.program_id(1)
    q0 = i * tq
    row = lax.broadcasted_iota(jnp.int32, (tq, n_rows), 0)
    col = lax.broadcasted_iota(jnp.int32, (tq, n_rows), 1)
    dist = (q0 + row - (col * NSA_CMP_STRIDE + (NSA_CMP_LEN - 1))).astype(F32)
    valid = dist >= 0.0
    lane = lax.broadcasted_iota(jnp.int32, (tq, LANES), 1)

    blk_i = lax.broadcasted_iota(jnp.int32, (n_slc, tq), 0)
    qpos = q0 + lax.broadcasted_iota(jnp.int32, (n_slc, tq), 1)
    q_blk = lax.shift_right_logical(qpos, int(math.log2(NSA_SLC_LEN)))
    forced = (blk_i == 0) | (blk_i == q_blk) | (blk_i == q_blk - 1)
    causal_blk = blk_i * NSA_SLC_LEN <= qpos
    blk_f = blk_i.astype(F32)

    for g in range(NSA_GROUPS):
        kv = kvc_ref[g]
        imp_t = jnp.zeros((n_slc, tq), F32)
        outs = []
        for hh in range(NSA_HPG):
            h = g * NSA_HPG + hh
            qh = _head_query(q_ref[:, (h // 2) * LANES:(h // 2 + 1) * LANES], h % 2, lane)
            s = _dot_nt(qh, kv) - slope_ref[h] * dist
            s = jnp.where(valid, s, NEG_INF)
            m = jnp.max(s, axis=1, keepdims=True)
            p = jnp.where(valid, jnp.exp(s - m), 0.0)
            l = jnp.sum(p, axis=1, keepdims=True)
            inv = jnp.where(l > 0.0, 1.0 / l, 0.0)
            pb = (p * inv).astype(BF16)
            o = _dot(pb, kv)
            imp_t = imp_t + _dot_nt(ovt_ref[...], pb)
            outs.append(o * gates_ref[:, g * LANES + 3 * hh:g * LANES + 3 * hh + 1])
        for pr in range(NSA_HPG // 2):
            c0 = (g * NSA_HPG // 2 + pr) * LANES
            oc_ref[:, c0:c0 + LANES] = _merge_head_pair(outs[2 * pr], outs[2 * pr + 1], lane)

        imp = jnp.where(causal_blk, jnp.where(forced, FORCED_SCORE, imp_t), NEG_INF)
        sel = jnp.zeros((n_slc, tq), F32)
        for _ in range(k_sel):
            mx = jnp.max(imp, axis=0, keepdims=True)
            cand = jnp.where(imp == mx, blk_f, float(n_slc))
            pick = blk_f == jnp.min(cand, axis=0, keepdims=True)
            sel = jnp.where(pick, 1.0, sel)
            imp = jnp.where(pick, -3e38, imp)
        sel_b = jnp.where(causal_blk, sel, 0.0).astype(BF16)
        sel_ref[g] = sel_b
        ones = jnp.ones((8, sub_tile), BF16)
        for sub in range(tq // sub_tile):
            r = sub * NSA_GROUPS + g
            cnt = _dot_nt(ones, sel_b[:, sub * sub_tile:(sub + 1) * sub_tile])
            any_ref[r:r + 1, :] = cnt[0:1]


def _nsa_cmp_topk(qkv, kvc, gates, ovt, tq, sub_tile):
    b, t, _ = qkv.shape
    g = NSA_GROUPS
    n_rows = kvc.shape[2]
    n_slc = t // NSA_SLC_LEN
    k_sel = min(NSA_SLC_TOPK, n_slc)
    nq = t // tq
    assert n_slc == LANES and (tq // sub_tile) * g == 8
    slopes = jnp.asarray(_alibi_slopes(NSA_HEADS))
    return pl.pallas_call(
        functools.partial(_nsa_cmp_topk_kernel, tq=tq, n_rows=n_rows, n_slc=n_slc, k_sel=k_sel,
                          sub_tile=sub_tile),
        out_shape=[
            jax.ShapeDtypeStruct((b, t, NSA_HEADS * NSA_HEAD_DIM), F32),
            jax.ShapeDtypeStruct((b, g, n_slc, t), BF16),
            jax.ShapeDtypeStruct((b, nq, 8, n_slc), F32),
        ],
        grid=(b, nq),
        in_specs=[
            pl.BlockSpec(memory_space=pltpu.SMEM),
            pl.BlockSpec((None, tq, NSA_HEADS * NSA_HEAD_DIM), lambda bb, i: (bb, i, 0)),
            pl.BlockSpec((None, g, n_rows, LANES), lambda bb, i: (bb, 0, 0, 0)),
            pl.BlockSpec((None, tq, g * LANES), lambda bb, i: (bb, i, 0)),
            pl.BlockSpec((n_slc, n_rows), lambda bb, i: (0, 0)),
        ],
        out_specs=[
            pl.BlockSpec((None, tq, NSA_HEADS * NSA_HEAD_DIM), lambda bb, i: (bb, i, 0)),
            pl.BlockSpec((None, g, n_slc, tq), lambda bb, i: (bb, 0, 0, i)),
            pl.BlockSpec((None, None, 8, n_slc), lambda bb, i: (bb, i, 0, 0)),
        ],
        compiler_params=_cparams(("arbitrary", "arbitrary")),
        name="nsa_cmp_topk",
    )(slopes, qkv, kvc, gates, ovt)


def _nsa_sel_win_kernel(bits_ref, slope_ref, q_ref, kvs_ref, vst_ref, kvw_ref, vwt_ref, selt_ref, oc_ref,
                        gates_ref, eselt_ref, o_ref,
                        list_ref, sd_ref, s_ref, mc_ref, st_ref, acc_ref, *, tq, tk, win_span, n_chunks):
    b = pl.program_id(0)
    g = pl.program_id(1)
    i = pl.program_id(2)
    q0 = i * tq
    hpg = NSA_HPG
    d = NSA_HEAD_DIM
    wide = hpg * tq
    lane = lax.broadcasted_iota(jnp.int32, (tq, LANES), 1)
    slopes = [slope_ref[g * hpg + hh] for hh in range(hpg)]

    def per_head_row(vals):
        return jnp.concatenate([jnp.full((1, tq), v, F32) for v in vals], axis=1)

    q4 = jnp.concatenate(
        [_head_query(q_ref[:, (hh // 2) * LANES:(hh // 2 + 1) * LANES], hh % 2, lane) for hh in range(hpg)],
        axis=0)

    rel = (lax.broadcasted_iota(jnp.int32, (tk, tq), 1)
           - lax.broadcasted_iota(jnp.int32, (tk, tq), 0))

    @pl.when(i == 0)
    def _():
        rel_f = rel.astype(F32)
        sd_ref[...] = jnp.concatenate([rel_f * slopes[hh] for hh in range(hpg)], axis=1)

    bits = bits_ref[(b * pl.num_programs(1) + g) * pl.num_programs(2) + i]

    def collect(j, n):
        hit = lax.shift_right_logical(bits, j) & 1

        @pl.when(hit == 1)
        def _():
            list_ref[n] = j
        return n + hit

    n_act = lax.fori_loop(0, n_chunks, collect, 0)

    st_ref[0:1, :] = jnp.full((1, wide), M_FLOOR, F32)
    st_ref[1:2, :] = jnp.zeros((1, wide), F32)
    acc_ref[...] = jnp.zeros(acc_ref.shape, F32)
    selt = selt_ref[...]

    def scores(j):
        kv = kvs_ref[pl.ds(pl.multiple_of(j * tk, tk), tk), :]
        return _dot_nt(kv, q4)

    def produce(j, raw, slot):
        picked = _dot(eselt_ref[j], selt)
        causal = rel + (q0 - j * tk) >= 0
        madd = jnp.where((picked > 0.5) & causal, 0.0, NEG_INF)
        s = raw - sd_ref[...] + jnp.concatenate([madd] * hpg, axis=1)
        s_ref[slot] = s
        mc_ref[slot, 0:1, :] = jnp.max(s, axis=0, keepdims=True)

    def consume(j, slot):
        off = per_head_row([slopes[hh] * (q0 - j * tk).astype(F32) for hh in range(hpg)])
        m_prev = st_ref[0:1, :]
        l_prev = st_ref[1:2, :]
        m_next = jnp.maximum(m_prev, mc_ref[slot, 0:1, :] - off)
        alpha = jnp.exp(m_prev - m_next)
        p = jnp.exp(s_ref[slot] - (m_next + off))
        st_ref[0:1, :] = m_next
        st_ref[1:2, :] = alpha * l_prev + jnp.sum(p, axis=0, keepdims=True)
        vt = vst_ref[:, pl.ds(pl.multiple_of(j * tk, tk), tk)]
        acc_ref[...] = alpha * acc_ref[...] + _dot(vt, p.astype(BF16))

    j_first = list_ref[0]
    produce(j_first, scores(j_first), 0)

    def body(t, carry):
        j_new = list_ref[t]
        raw = scores(j_new)
        consume(list_ref[t - 1], (t - 1) & 1)
        produce(j_new, raw, t & 1)
        return carry

    lax.fori_loop(1, n_act, body, 0)
    consume(list_ref[n_act - 1], (n_act - 1) & 1)

    start = pl.multiple_of(jnp.maximum(q0 + tq - win_span, 0), tq)
    kvw = kvw_ref[pl.ds(start, win_span), :]
    s_w = _dot_nt(kvw, q4)
    dist_w = (lax.broadcasted_iota(jnp.int32, (win_span, tq), 1)
              - lax.broadcasted_iota(jnp.int32, (win_span, tq), 0) + (q0 - start))
    valid_w = (dist_w >= 0) & (dist_w < NSA_WINDOW)
    dist_wf = dist_w.astype(F32)
    s_w = s_w + jnp.concatenate(
        [jnp.where(valid_w, -slopes[hh] * dist_wf, NEG_INF) for hh in range(hpg)], axis=1)
    p_w = jnp.exp(s_w - jnp.max(s_w, axis=0, keepdims=True))
    p_w = p_w * (1.0 / jnp.sum(p_w, axis=0, keepdims=True))
    o_w = _dot(vwt_ref[:, pl.ds(start, win_span)], p_w.astype(BF16))

    l_s = st_ref[1:2, :]
    o_s = acc_ref[...] * jnp.where(l_s > 0.0, 1.0 / l_s, 0.0)
    gates_t = gates_ref[...].T
    g_s = jnp.concatenate([gates_t[3 * hh + 1:3 * hh + 2] for hh in range(hpg)], axis=1)
    g_w = jnp.concatenate([gates_t[3 * hh + 2:3 * hh + 3] for hh in range(hpg)], axis=1)
    comb = g_s * o_s + g_w * o_w
    by_head = jnp.concatenate([comb[:, hh * tq:(hh + 1) * tq] for hh in range(hpg)], axis=0)
    o_ref[...] = (by_head.T + oc_ref[...]).astype(o_ref.dtype)


def _nsa_sel_win(bits, qkv, vst, vwt, selt, oc, gates, eselt, tq, tk):
    b, t, _ = qkv.shape
    g = NSA_GROUPS
    d = NSA_HEAD_DIM
    n_slc = t // NSA_SLC_LEN
    nq = t // tq
    hd = NSA_HPG * d
    wide = NSA_HPG * tq
    kv_col0 = NSA_HEADS * d // LANES
    win_span = min(NSA_WINDOW + tq, t)
    n_chunks = t // tk
    slopes = jnp.asarray(_alibi_slopes(NSA_HEADS))
    grid_spec = pltpu.PrefetchScalarGridSpec(
        num_scalar_prefetch=1,
        grid=(b, g, nq),
        in_specs=[
            pl.BlockSpec(memory_space=pltpu.SMEM),
            pl.BlockSpec((None, tq, hd), lambda bb, gg, i, bits: (bb, i, gg)),
            pl.BlockSpec((None, t, LANES), lambda bb, gg, i, bits: (bb, 0, kv_col0 + gg)),
            pl.BlockSpec((None, d, t), lambda bb, gg, i, bits: (bb, gg, 0)),
            pl.BlockSpec((None, t, LANES), lambda bb, gg, i, bits: (bb, 0, kv_col0 + g + gg)),
            pl.BlockSpec((None, d, t), lambda bb, gg, i, bits: (bb, gg, 0)),
            pl.BlockSpec((None, None, n_slc, tq), lambda bb, gg, i, bits: (bb, gg, 0, i)),
            pl.BlockSpec((None, tq, hd), lambda bb, gg, i, bits: (bb, i, gg)),
            pl.BlockSpec((None, tq, LANES), lambda bb, gg, i, bits: (bb, i, gg)),
            pl.BlockSpec(eselt.shape, lambda bb, gg, i, bits: (0, 0, 0)),
        ],
        out_specs=pl.BlockSpec((None, tq, hd), lambda bb, gg, i, bits: (bb, i, gg)),
        scratch_shapes=[
            pltpu.SMEM((n_chunks,), jnp.int32),
            pltpu.VMEM((tk, wide), F32),
            pltpu.VMEM((2, tk, wide), F32),
            pltpu.VMEM((2, 8, wide), F32),
            pltpu.VMEM((8, wide), F32),
            pltpu.VMEM((d, wide), F32),
        ],
    )
    return pl.pallas_call(
        functools.partial(_nsa_sel_win_kernel, tq=tq, tk=tk, win_span=win_span, n_chunks=n_chunks),
        out_shape=jax.ShapeDtypeStruct((b, t, NSA_HEADS * d), BF16),
        grid_spec=grid_spec,
        compiler_params=_cparams(("arbitrary", "arbitrary", "arbitrary")),
        name="nsa_sel_win",
    )(bits, slopes, qkv, qkv, vst, qkv, vwt, selt, oc, gates, eselt)


def _nsa_weight_layout():
    hd, g, d = NSA_HEADS * NSA_HEAD_DIM, NSA_GROUPS, NSA_HEAD_DIM
    base = {name: hd + n * g * d for n, name in enumerate(["kc", "vc", "ks", "vs", "kw", "vw"])}
    gate0 = hd + 6 * g * d
    cols = list(range(hd))
    for kn, vn in (("ks", "vs"), ("kw", "vw"), ("kc", "vc")):
        for gg in range(g):
            cols += list(range(base[kn] + gg * d, base[kn] + (gg + 1) * d))
            cols += list(range(base[vn] + gg * d, base[vn] + (gg + 1) * d))
    for gg in range(g):
        n_gate = NSA_HPG * 3
        cols += list(range(gate0 + gg * n_gate, gate0 + (gg + 1) * n_gate)) + [-1] * (LANES - n_gate)
    return np.asarray(cols, np.int32)


def _permute_columns(w, cols):
    picked = jnp.take(w, jnp.asarray(np.maximum(cols, 0)), axis=1)
    return jnp.where(jnp.asarray(cols >= 0)[None, :], picked, 0.0)


def _overlap_t(n_rows, n_slc):
    n_cmp = n_rows - 1
    c_start = np.arange(n_rows) * NSA_CMP_STRIDE
    s_start = np.arange(n_slc) * NSA_SLC_LEN
    ov = (np.minimum(c_start[:, None] + NSA_CMP_LEN, s_start[None, :] + NSA_SLC_LEN)
          - np.maximum(c_start[:, None], s_start[None, :]))
    ov = np.clip(ov, 0, None).astype(np.float32) / NSA_CMP_LEN
    ov[n_cmp:] = 0.0
    return jnp.asarray(ov.T, BF16)


def _block_expander(n_slc, tk):
    n_chunks = n_slc * NSA_SLC_LEN // tk
    key_blk = (np.arange(n_chunks)[:, None] * tk + np.arange(tk)[None, :]) // NSA_SLC_LEN
    e = (key_blk[:, :, None] == np.arange(n_slc)[None, None, :]).astype(np.float32)
    return jnp.asarray(e, BF16)


def _chunk_bits(any_sel, sub_per_tile, blocks_per_chunk):
    b, nq, _, n_slc = any_sel.shape
    g = NSA_GROUPS
    n_chunks = n_slc // blocks_per_chunk
    a = any_sel.reshape(b, nq, sub_per_tile, g, n_chunks, blocks_per_chunk).max(axis=-1) > 0.5
    a = a.transpose(0, 3, 1, 2, 4).reshape(b * g * nq * sub_per_tile, n_chunks)
    weights = jnp.left_shift(jnp.uint32(1), jnp.arange(n_chunks, dtype=jnp.uint32))
    return lax.bitcast_convert_type(jnp.sum(jnp.where(a, weights[None, :], jnp.uint32(0)), axis=1,
                                            dtype=jnp.uint32), jnp.int32)


def _diff_mixer(x, mod, layer, w_in, w_out, lam_q1, lam_k1, lam_q2, lam_k2, subln_g):
    n_qk = 2 * DIFF_HEADS * 2 * DIFF_HEAD_DIM
    n_v = DIFF_HEADS * DIFF_V_DIM
    w = w_in.astype(BF16)
    qk, vt = _modulated_projection(x, mod, layer, w[:, :n_qk], [(0, n_qk, "id")], [BF16],
                                   w[:, n_qk:].T, [(0, n_v)])
    lam_vecs = jnp.stack([lam_q1, lam_k1, lam_q2, lam_k2]).astype(F32)
    o = _diff_attention(qk, vt, lam_vecs, subln_g.astype(F32), layer)
    return o, w_out.astype(BF16)


def _nsa_mixer(x, mod, layer, w_in, pos_k, pos_v, k_w1, k_w2, v_w1, v_w2, w_out):
    b, t, _ = x.shape
    g, d = NSA_GROUPS, NSA_HEAD_DIM
    hd = NSA_HEADS * d
    cols = _nsa_weight_layout()
    w = _permute_columns(w_in, cols).astype(BF16)
    n_main = hd + 2 * (2 * g * d)
    n_cmp_cols = 2 * g * d
    plan = [(0, n_main, "id"), (n_main, n_main + n_cmp_cols, "id"),
            (n_main + n_cmp_cols, n_main + n_cmp_cols + g * LANES, "sigmoid")]
    v_cols = np.concatenate([np.arange(hd + 6 * g * d)[hd + (2 * n + 1) * g * d: hd + (2 * n + 2) * g * d]
                             for n in (1, 2)])
    wt = jnp.take(w_in, jnp.asarray(v_cols), axis=1).T.astype(BF16)
    qkv, kvc_tok, gates, vst, vwt = _modulated_projection(
        x, mod, layer, w, plan, [BF16, F32, F32], wt, [(0, g * d), (g * d, 2 * g * d)])

    rows = t // NSA_CMP_STRIDE
    r = kvc_tok.reshape(b, rows, NSA_CMP_STRIDE, g, 2, d).transpose(0, 3, 4, 1, 2, 5)
    r = r.reshape(b, g, 2, rows, NSA_CMP_STRIDE * d)
    pos = jnp.stack([pos_k.reshape(1, -1), pos_v.reshape(1, -1)]).astype(F32)
    w1 = jnp.stack([k_w1, v_w1]).astype(BF16)
    zeros = jnp.zeros((NSA_CMP_HIDDEN, d), F32)
    w2pad = jnp.stack([jnp.concatenate([k_w2, zeros], axis=1),
                       jnp.concatenate([zeros, v_w2], axis=1)]).astype(BF16)
    kvc = _compress(r, pos, w1, w2pad)

    n_slc = t // NSA_SLC_LEN
    tq_sel = min(SEL_Q_TILE, t)
    tq_cmp = min(CMP_Q_TILE, t)
    tk = min(SEL_K_TILE, t)
    oc, selt, any_sel = _nsa_cmp_topk(qkv, kvc, gates, _overlap_t(rows, n_slc), tq_cmp, tq_sel)
    bits = _chunk_bits(any_sel, tq_cmp // tq_sel, tk // NSA_SLC_LEN)
    o = _nsa_sel_win(bits, qkv, vst, vwt, selt, oc, gates, _block_expander(n_slc, tk), tq_sel, tk)
    return o, w_out.astype(BF16)


def kernel(x, c, diff_w_in, diff_w_out, diff_lam_q1, diff_lam_k1, diff_lam_q2, diff_lam_k2, diff_subln_g,
           nsa_w_in, nsa_cmp_pos_k, nsa_cmp_pos_v, nsa_cmp_k_w1, nsa_cmp_k_w2, nsa_cmp_v_w1, nsa_cmp_v_w2,
           nsa_w_out, ada_w, ada_b, ln_mix_g, ln_mix_b, ln_ffn_g, ln_ffn_b, ffn_w_gu, ffn_w_down):
    mod = _ada_modulation(c, ada_w, ada_b)
    for i in range(DEPTH):
        j = i // N_MIXERS
        if i % N_MIXERS == 0:
            o, w_out = _diff_mixer(x, mod, i, diff_w_in[j], diff_w_out[j], diff_lam_q1[j], diff_lam_k1[j],
                                   diff_lam_q2[j], diff_lam_k2[j], diff_subln_g[j])
        else:
            o, w_out = _nsa_mixer(x, mod, i, nsa_w_in[j], nsa_cmp_pos_k[j], nsa_cmp_pos_v[j],
                                  nsa_cmp_k_w1[j], nsa_cmp_k_w2[j], nsa_cmp_v_w1[j], nsa_cmp_v_w2[j],
                                  nsa_w_out[j])
        x = _outproj_ln(o, w_out, x, mod, i, ln_mix_g[i], ln_mix_b[i])
        x = _ffn_ln(x, mod, i, ffn_w_gu[i].astype(BF16), ffn_w_down[i].astype(BF16), ln_ffn_g[i], ln_ffn_b[i])
    return x
```

```python
import functools
import math

import numpy as np
import jax
import jax.numpy as jnp
from jax import lax
from jax.experimental import pallas as pl
from jax.experimental.pallas import tpu as pltpu

F32 = jnp.float32
BF16 = jnp.bfloat16

D_MODEL = 1024
DEPTH = 2
N_MIXERS = 2
DEEPNORM_ALPHA = (2 * DEPTH) ** 0.25
LN_EPS = 1e-5
RMS_EPS = 1e-5
NEG_INF = -1e30
FORCED_SCORE = 1e6

DIFF_HEADS = 8
DIFF_HEAD_DIM = 64
DIFF_V_DIM = 2 * DIFF_HEAD_DIM

NSA_HEADS = 16
NSA_GROUPS = 4
NSA_HEAD_DIM = 64
NSA_HPG = NSA_HEADS // NSA_GROUPS
NSA_CMP_LEN = 32
NSA_CMP_STRIDE = 16
NSA_CMP_HIDDEN = 128
NSA_SLC_LEN = 64
NSA_SLC_TOPK = 16
NSA_WINDOW = 512

FFN_HIDDEN = -(-8 * D_MODEL // (3 * 256)) * 256

LANES = 128
VMEM_LIMIT = 56 * 1024 * 1024
PROJ_ROWS = 512
FFN_CHUNK = FFN_HIDDEN // 2
DIFF_TILE = 256
DIFF_STREAMS = 2
DIFF_ELIDE_LOG = 110.0
DIFF_ELIDE_SLACK = 1.02
CMP_Q_TILE = 256
SEL_Q_TILE = 128
SEL_K_TILE = 256
M_FLOOR = -1e29
TAKEN = -3e38


def _cparams(sem):
    return pltpu.CompilerParams(dimension_semantics=sem, vmem_limit_bytes=VMEM_LIMIT)


def _dot(a, b):
    return jnp.dot(a, b, preferred_element_type=F32)


def _dot_nt(a, b):
    return lax.dot_general(a, b, (((1,), (1,)), ((), ())), preferred_element_type=F32)


def _layer_norm(z, g, b):
    mu = jnp.mean(z, axis=-1, keepdims=True)
    zc = z - mu
    var = jnp.mean(zc * zc, axis=-1, keepdims=True)
    return zc * lax.rsqrt(var + LN_EPS) * g + b


def _alibi_slopes(n_heads):
    return (2.0 ** (-8.0 * np.arange(1, n_heads + 1, dtype=np.float32) / n_heads)).astype(np.float32)


def _ada_kernel(c_ref, w_ref, b_ref, o_ref):
    c = c_ref[...]
    s = (c * jax.nn.sigmoid(c)).astype(BF16)
    o_ref[0] = _dot(s, w_ref[0].astype(BF16)) + b_ref[0]


def _ada_modulation(c, ada_w, ada_b):
    depth, d, d6 = ada_w.shape
    b = c.shape[0]
    rows = 8
    c_pad = jnp.zeros((rows, d), F32).at[:b].set(c)
    out = pl.pallas_call(
        _ada_kernel,
        out_shape=jax.ShapeDtypeStruct((depth, rows, d6), F32),
        grid=(depth, d6 // d),
        in_specs=[
            pl.BlockSpec((rows, d), lambda l, j: (0, 0)),
            pl.BlockSpec((1, d, d), lambda l, j: (l, 0, j)),
            pl.BlockSpec((1, 1, d), lambda l, j: (l, 0, j)),
        ],
        out_specs=pl.BlockSpec((1, rows, d), lambda l, j: (l, 0, j)),
        compiler_params=_cparams(("arbitrary", "arbitrary")),
        name="ada_modulation",
    )(c_pad, ada_w, ada_b.reshape(depth, 1, d6))
    return out[:, :b].reshape(depth, b, 6, 1, d)


def _mod_spec(layer, which, n_grid):
    d = D_MODEL
    if n_grid == 2:
        return pl.BlockSpec((None, None, None, 1, d), lambda b, i: (layer, b, which, 0, 0))
    return pl.BlockSpec((None, None, None, 1, d), lambda b, i, k: (layer, b, which, 0, 0))


def _proj_kernel(x_ref, sc_ref, sh_ref, w_ref, wt_ref, *out_refs, plan, t_plan):
    h = (x_ref[...] * (1.0 + sc_ref[...]) + sh_ref[...]).astype(BF16)
    for ref, (c0, c1, kind), in zip(out_refs, plan):
        step = 1024
        for s0 in range(c0, c1, step):
            s1 = min(s0 + step, c1)
            y = _dot(h, w_ref[:, s0:s1])
            if kind == "sigmoid":
                y = jax.nn.sigmoid(y)
            ref[:, s0 - c0:s1 - c0] = y.astype(ref.dtype)
    for ref, (r0, r1) in zip(out_refs[len(plan):], t_plan):
        ref[...] = _dot_nt(wt_ref[r0:r1, :], h).astype(ref.dtype)


def _modulated_projection(x, mod, layer, w, plan, out_dtypes, wt, t_plan):
    b, t, d = x.shape
    n = w.shape[1]
    tm = min(PROJ_ROWS, t)
    out_shape = [jax.ShapeDtypeStruct((b, t, c1 - c0), dt) for (c0, c1, _), dt in zip(plan, out_dtypes)]
    out_specs = [pl.BlockSpec((None, tm, c1 - c0), lambda bb, i: (bb, i, 0)) for (c0, c1, _) in plan]
    out_shape += [jax.ShapeDtypeStruct((b, r1 - r0, t), BF16) for (r0, r1) in t_plan]
    out_specs += [pl.BlockSpec((None, r1 - r0, tm), lambda bb, i: (bb, 0, i)) for (r0, r1) in t_plan]
    return pl.pallas_call(
        functools.partial(_proj_kernel, plan=plan, t_plan=t_plan),
        out_shape=out_shape,
        grid=(b, t // tm),
        in_specs=[
            pl.BlockSpec((None, tm, d), lambda bb, i: (bb, i, 0)),
            _mod_spec(layer, 1, 2),
            _mod_spec(layer, 0, 2),
            pl.BlockSpec((d, n), lambda bb, i: (0, 0)),
            pl.BlockSpec(wt.shape, lambda bb, i: (0, 0)),
        ],
        out_specs=out_specs,
        compiler_params=_cparams(("arbitrary", "arbitrary")),
        name=f"mod_proj_l{layer}",
    )(x, mod, mod, w, wt)


def _diff_attn_kernel(slope_ref, lam_ref, g_ref, q_ref, k_ref, vt_ref, o_ref, kn_ref, rel_ref, *scratch,
                      tile, lam_init, n_streams):
    h = pl.program_id(1)
    qi = pl.program_id(2)
    slope = slope_ref[0, h]
    inv_slope = slope_ref[1, h]
    half = DIFF_HEAD_DIM
    s_refs, mc_refs, st_refs, acc_refs = (scratch[n::4] for n in range(4))

    key = lax.broadcasted_iota(jnp.int32, (tile, tile), 0)
    qry = lax.broadcasted_iota(jnp.int32, (tile, tile), 1)
    rel_ref[...] = (qry - key).astype(F32) * slope
    lane = lax.broadcasted_iota(jnp.int32, (tile, LANES), 1)

    @pl.when(qi == 0)
    def _():
        for n in range(n_streams):
            kf = k_ref[n].astype(F32)
            kn_ref[n] = jnp.max(jnp.sqrt(jnp.max(jnp.sum(kf * kf, axis=1, keepdims=True),
                                                 axis=0, keepdims=True)))

    qcs = []
    reach_max = jnp.float32(0.0)
    for n in range(n_streams):
        qf = q_ref[n].astype(F32) * (DIFF_HEAD_DIM ** -0.5)
        qs = qf.astype(BF16)
        zero = jnp.zeros_like(qs)
        qcs.append((jnp.where(lane < half, qs, zero), jnp.where(lane >= half, qs, zero)))
        qn = jnp.max(jnp.sqrt(jnp.max(jnp.sum(qf * qf, axis=1, keepdims=True), axis=0, keepdims=True)))
        reach = (DIFF_ELIDE_LOG + DIFF_ELIDE_SLACK * 2.0 * qn * kn_ref[n]) * inv_slope
        reach_max = jnp.maximum(reach_max, reach)
        acc_refs[n][...] = jnp.zeros(acc_refs[n].shape, F32)
        for c in range(2):
            st_refs[n][2 * c:2 * c + 1, :] = jnp.full((1, tile), M_FLOOR, F32)
            st_refs[n][2 * c + 1:2 * c + 2, :] = jnp.zeros((1, tile), F32)
    n_keep = lax.fori_loop(
        0, qi, lambda a, cnt: cnt + jnp.where((a * tile + 1).astype(F32) < reach_max, 1, 0), 0)
    j0 = qi - n_keep

    def scores(n, j):
        k = k_ref[n, pl.ds(pl.multiple_of(j * tile, tile), tile), :]
        return [_dot_nt(k, qcs[n][c]) for c in range(2)]

    def produce(n, j, raw, diagonal):
        slot = j & 1
        for c in range(2):
            s = raw[c] - rel_ref[...]
            if diagonal is True:
                s = jnp.where(key <= qry, s, NEG_INF)
            elif diagonal is not False:
                s = jnp.where((key <= qry) | jnp.logical_not(diagonal), s, NEG_INF)
            s_refs[n][slot, c] = s
            mc_refs[n][slot, c:c + 1, :] = jnp.max(s, axis=0, keepdims=True)

    def consume(n, j):
        slot = j & 1
        vt = vt_ref[n, :, pl.ds(pl.multiple_of(j * tile, tile), tile)]
        off = slope * ((qi - j) * tile).astype(F32)
        st_ref, acc_ref = st_refs[n], acc_refs[n]
        for c in range(2):
            m_prev = st_ref[2 * c:2 * c + 1, :]
            l_prev = st_ref[2 * c + 1:2 * c + 2, :]
            m_next = jnp.maximum(m_prev, mc_refs[n][slot, c:c + 1, :] - off)
            alpha = jnp.exp(m_prev - m_next)
            p = jnp.exp(s_refs[n][slot, c] - (m_next + off))
            st_ref[2 * c:2 * c + 1, :] = m_next
            st_ref[2 * c + 1:2 * c + 2, :] = alpha * l_prev + jnp.sum(p, axis=0, keepdims=True)
            acc_ref[c] = alpha * acc_ref[c] + _dot(vt, p.astype(BF16))

    for n in range(n_streams):
        produce(n, j0, scores(n, j0), j0 == qi)

    def step(t, diagonal):
        raws = [scores(n, t) for n in range(n_streams)]
        for n in range(n_streams):
            consume(n, t - 1)
            produce(n, t, raws[n], diagonal)

    def body(t, carry):
        step(t, False)
        return carry

    lax.fori_loop(j0 + 1, qi, body, 0)

    @pl.when(qi > j0)
    def _():
        step(qi, True)

    lam_v = lam_ref[...]
    lam = (jnp.exp(jnp.sum(lam_v[0:1] * lam_v[1:2], axis=1, keepdims=True))
           - jnp.exp(jnp.sum(lam_v[2:3] * lam_v[3:4], axis=1, keepdims=True)) + lam_init)
    for n in range(n_streams):
        consume(n, qi)
        l0 = st_refs[n][1:2, :]
        l1 = st_refs[n][3:4, :]
        o = acc_refs[n][0] / l0 - lam * (acc_refs[n][1] / l1)
        o = o * lax.rsqrt(jnp.mean(o * o, axis=0, keepdims=True) + RMS_EPS)
        o_ref[n] = ((o * g_ref[...] * (1.0 - lam_init)).T).astype(o_ref.dtype)


def _diff_attention(qk, vt, lam_vecs, subln_g, layer_idx):
    b, t, _ = qk.shape
    hh = DIFF_HEADS
    tile = min(DIFF_TILE, t)
    ns = DIFF_STREAMS if b % DIFF_STREAMS == 0 else 1
    lam_init = 0.8 - 0.6 * math.exp(-0.3 * layer_idx)
    slopes = _alibi_slopes(hh)
    slope_tab = jnp.asarray(np.stack([slopes, 1.0 / slopes]))
    per_stream = [
        pltpu.VMEM((2, 2, tile, tile), F32),
        pltpu.VMEM((2, 8, tile), F32),
        pltpu.VMEM((8, tile), F32),
        pltpu.VMEM((2, DIFF_V_DIM, tile), F32),
    ]
    return pl.pallas_call(
        functools.partial(_diff_attn_kernel, tile=tile, lam_init=lam_init, n_streams=ns),
        out_shape=jax.ShapeDtypeStruct((b, t, hh * DIFF_V_DIM), BF16),
        grid=(b // ns, hh, t // tile),
        in_specs=[
            pl.BlockSpec(memory_space=pltpu.SMEM),
            pl.BlockSpec((4, DIFF_HEAD_DIM), lambda bb, h, i: (0, 0)),
            pl.BlockSpec((DIFF_V_DIM, 1), lambda bb, h, i: (0, 0)),
            pl.BlockSpec((ns, tile, LANES), lambda bb, h, i: (bb, i, h)),
            pl.BlockSpec((ns, t, LANES), lambda bb, h, i: (bb, 0, hh + h)),
            pl.BlockSpec((ns, DIFF_V_DIM, t), lambda bb, h, i: (bb, h, 0)),
        ],
        out_specs=pl.BlockSpec((ns, tile, LANES), lambda bb, h, i: (bb, i, h)),
        scratch_shapes=[pltpu.SMEM((ns,), F32), pltpu.VMEM((tile, tile), F32)] + per_stream * ns,
        compiler_params=_cparams(("arbitrary", "arbitrary", "arbitrary")),
        name="diff_attention",
    )(slope_tab, lam_vecs, subln_g.reshape(DIFF_V_DIM, 1), qk, qk, vt)


def _mix_ffn_kernel(o_ref, wo_ref, x_ref, ga_ref, lng_a_ref, lnb_a_ref, sc_ref, sh_ref, gf_ref,
                    wg_ref, wu_ref, wd_ref, lng_f_ref, lnb_f_ref, out_ref, x1_ref, h_ref, acc_ref):
    k = pl.program_id(2)

    @pl.when(k == 0)
    def _():
        y = _dot(o_ref[...], wo_ref[...])
        z = DEEPNORM_ALPHA * x_ref[...] + (1.0 + ga_ref[...]) * y
        x1 = _layer_norm(z, lng_a_ref[...], lnb_a_ref[...])
        x1_ref[...] = x1
        h_ref[...] = (x1 * (1.0 + sc_ref[...]) + sh_ref[...]).astype(BF16)
        acc_ref[...] = jnp.zeros(acc_ref.shape, F32)

    h = h_ref[...]
    gp = _dot(h, wg_ref[...])
    up = _dot(h, wu_ref[...])
    act = (gp * jax.nn.sigmoid(gp) * up).astype(BF16)
    acc_ref[...] += _dot(act, wd_ref[...])

    @pl.when(k == pl.num_programs(2) - 1)
    def _():
        z = DEEPNORM_ALPHA * x1_ref[...] + (1.0 + gf_ref[...]) * acc_ref[...]
        out_ref[...] = _layer_norm(z, lng_f_ref[...], lnb_f_ref[...])


def _mix_ffn(o, w_out, x, mod, layer, ln_mix_g, ln_mix_b, w_gu, w_down, ln_ffn_g, ln_ffn_b):
    b, t, d = x.shape
    dh = o.shape[-1]
    fh = w_down.shape[0]
    th = min(FFN_CHUNK, fh)
    nh = fh // th
    tm = min(PROJ_ROWS, t)
    row_vec = pl.BlockSpec((1, d), lambda bb, i, k: (0, 0))
    return pl.pallas_call(
        _mix_ffn_kernel,
        out_shape=jax.ShapeDtypeStruct((b, t, d), F32),
        grid=(b, t // tm, nh),
        in_specs=[
            pl.BlockSpec((None, tm, dh), lambda bb, i, k: (bb, i, 0)),
            pl.BlockSpec((dh, d), lambda bb, i, k: (0, 0)),
            pl.BlockSpec((None, tm, d), lambda bb, i, k: (bb, i, 0)),
            _mod_spec(layer, 2, 3),
            row_vec,
            row_vec,
            _mod_spec(layer, 4, 3),
            _mod_spec(layer, 3, 3),
            _mod_spec(layer, 5, 3),
            pl.BlockSpec((d, th), lambda bb, i, k: (0, k)),
            pl.BlockSpec((d, th), lambda bb, i, k: (0, nh + k)),
            pl.BlockSpec((th, d), lambda bb, i, k: (k, 0)),
            row_vec,
            row_vec,
        ],
        out_specs=pl.BlockSpec((None, tm, d), lambda bb, i, k: (bb, i, 0)),
        scratch_shapes=[pltpu.VMEM((tm, d), F32), pltpu.VMEM((tm, d), BF16), pltpu.VMEM((tm, d), F32)],
        compiler_params=_cparams(("arbitrary", "arbitrary", "arbitrary")),
        name=f"mix_ffn_l{layer}",
    )(o, w_out, x, mod, ln_mix_g.reshape(1, d), ln_mix_b.reshape(1, d), mod, mod, mod,
      w_gu, w_gu, w_down, ln_ffn_g.reshape(1, d), ln_ffn_b.reshape(1, d))


def _gelu_tanh(x):
    return 0.5 * x * (1.0 + jnp.tanh(math.sqrt(2.0 / math.pi) * (x + 0.044715 * (x * x * x))))


def _compress_kernel(x_ref, pos_ref, w1_ref, w2_ref, o_ref, vt_ref):
    d = NSA_HEAD_DIM
    n_rows, stride, _ = x_ref.shape
    half = stride * d
    lane = lax.broadcasted_iota(jnp.int32, (n_rows, LANES), 1)
    pieces = ([], [])
    for lp in range(stride // 2):
        even = x_ref[:, 2 * lp, :]
        odd = x_ref[:, 2 * lp + 1, :]
        pieces[0].append(jnp.where(lane < d, even, pltpu.roll(odd, d, 1)))
        pieces[1].append(jnp.where(lane < d, pltpu.roll(even, d, 1), odd))
    out = None
    for kv in range(2):
        r = jnp.concatenate(pieces[kv], axis=1)
        r_next = pltpu.roll(r, n_rows - 1, 0)
        a = (r + pos_ref[kv, :, :half]).astype(BF16)
        bnx = (r_next + pos_ref[kv, :, half:]).astype(BF16)
        hid = _dot(a, w1_ref[kv, :half, :]) + _dot(bnx, w1_ref[kv, half:, :])
        y = _dot(_gelu_tanh(hid).astype(BF16), w2_ref[kv])
        out = y if out is None else out + y
    o_ref[...] = out.astype(o_ref.dtype)
    vt_ref[...] = out.T[NSA_HEAD_DIM:, :].astype(vt_ref.dtype)


def _compress(kvc_tok, pos, w1, w2pad):
    b, t, _ = kvc_tok.shape
    g = NSA_GROUPS
    rows = t // NSA_CMP_STRIDE
    width = NSA_CMP_STRIDE * NSA_HEAD_DIM
    return pl.pallas_call(
        _compress_kernel,
        out_shape=[jax.ShapeDtypeStruct((b, g, rows, LANES), BF16),
                   jax.ShapeDtypeStruct((b, g, NSA_HEAD_DIM, rows), BF16)],
        grid=(b, g),
        in_specs=[
            pl.BlockSpec((None, rows, NSA_CMP_STRIDE, LANES), lambda bb, gg: (bb, 0, 0, gg)),
            pl.BlockSpec((2, 1, 2 * width), lambda bb, gg: (0, 0, 0)),
            pl.BlockSpec((2, 2 * width, NSA_CMP_HIDDEN), lambda bb, gg: (0, 0, 0)),
            pl.BlockSpec((2, NSA_CMP_HIDDEN, LANES), lambda bb, gg: (0, 0, 0)),
        ],
        out_specs=[pl.BlockSpec((None, None, rows, LANES), lambda bb, gg: (bb, gg, 0, 0)),
                   pl.BlockSpec((None, None, NSA_HEAD_DIM, rows), lambda bb, gg: (bb, gg, 0, 0))],
        compiler_params=_cparams(("arbitrary", "arbitrary")),
        name="nsa_compress",
    )(kvc_tok.reshape(b, rows, NSA_CMP_STRIDE, g * LANES), pos, w1, w2pad)


def _head_query(q_pair, head_in_pair, lane):
    d = NSA_HEAD_DIM
    qf = q_pair.astype(F32) * (d ** -0.5)
    if head_in_pair == 1:
        qf = pltpu.roll(qf, d, 1)
    return jnp.where(lane < d, qf, 0.0).astype(BF16)


def _merge_head_pair(o_even, o_odd, lane):
    return jnp.where(lane < NSA_HEAD_DIM, pltpu.roll(o_even, NSA_HEAD_DIM, 1), o_odd)


def _nsa_cmp_topk_kernel(slope_ref, q_ref, kvc_ref, vct_ref, gates_ref, ovt_ref,
                         oc_ref, sel_ref, any_ref, sdc_ref, *, tq, n_rows, n_slc, k_sel, sub_tile):
    i = pl.program_id(1)
    q0 = i * tq
    hpg = NSA_HPG
    d = NSA_HEAD_DIM
    lane = lax.broadcasted_iota(jnp.int32, (tq, LANES), 1)
    qry_c = lax.broadcasted_iota(jnp.int32, (n_rows, tq), 1)
    end_c = lax.broadcasted_iota(jnp.int32, (n_rows, tq), 0) * NSA_CMP_STRIDE + (NSA_CMP_LEN - 1)

    @pl.when(i == 0)
    def _():
        rel_c = (qry_c - end_c).astype(F32)
        for h in range(NSA_HEADS):
            sdc_ref[:, h * tq:(h + 1) * tq] = rel_c * slope_ref[h]

    valid = end_c <= q0 + qry_c

    blk_i = lax.broadcasted_iota(jnp.int32, (n_slc, tq), 0)
    qpos = q0 + lax.broadcasted_iota(jnp.int32, (n_slc, tq), 1)
    q_blk = lax.shift_right_logical(qpos, int(math.log2(NSA_SLC_LEN)))
    forced = (blk_i == 0) | (blk_i == q_blk) | (blk_i == q_blk - 1)
    causal_blk = blk_i * NSA_SLC_LEN <= qpos
    blk_f = blk_i.astype(F32)

    for g in range(NSA_GROUPS):
        q4 = jnp.concatenate(
            [_head_query(q_ref[:, ((g * hpg + hh) // 2) * LANES:((g * hpg + hh) // 2 + 1) * LANES], hh % 2, lane)
             for hh in range(hpg)], axis=0)
        raw = _dot_nt(kvc_ref[g], q4)
        s = jnp.concatenate(
            [jnp.where(valid, raw[:, hh * tq:(hh + 1) * tq]
                       - sdc_ref[:, (g * hpg + hh) * tq:(g * hpg + hh + 1) * tq], NEG_INF)
             for hh in range(hpg)], axis=1)
        m = jnp.max(s, axis=0, keepdims=True)
        p = jnp.exp(s - m)
        l = jnp.sum(p, axis=0, keepdims=True)
        inv = jnp.where(m > 0.5 * NEG_INF, 1.0 / l, 0.0)
        lhs = jnp.concatenate([vct_ref[g], ovt_ref[...]], axis=0)
        r = _dot(lhs, p.astype(BF16))
        gates_t = gates_ref[:, g * LANES:(g + 1) * LANES].T
        w = inv * jnp.concatenate([gates_t[3 * hh:3 * hh + 1] for hh in range(hpg)], axis=1)
        oc_t = r[:d] * w
        by_head = jnp.concatenate([oc_t[:, hh * tq:(hh + 1) * tq] for hh in range(hpg)], axis=0)
        oc_ref[:, g * hpg * d:(g + 1) * hpg * d] = by_head.T
        imp_t = r[d:, 0:tq] * inv[:, 0:tq]
        for hh in range(1, hpg):
            imp_t = imp_t + r[d:, hh * tq:(hh + 1) * tq] * inv[:, hh * tq:(hh + 1) * tq]

        imp = jnp.where(causal_blk, jnp.where(forced, FORCED_SCORE, imp_t), NEG_INF)
        for _ in range(k_sel):
            mx = jnp.max(imp, axis=0, keepdims=True)
            cand = jnp.where(imp == mx, blk_f, float(n_slc))
            pick = blk_f == jnp.min(cand, axis=0, keepdims=True)
            imp = jnp.where(pick, TAKEN, imp)
        sel_b = jnp.where(causal_blk & (imp < 0.5 * TAKEN), 1.0, 0.0).astype(BF16)
        sel_ref[g] = sel_b
        ones = jnp.ones((8, sub_tile), BF16)
        for sub in range(tq // sub_tile):
            r = sub * NSA_GROUPS + g
            cnt = _dot_nt(ones, sel_b[:, sub * sub_tile:(sub + 1) * sub_tile])
            any_ref[r:r + 1, :] = cnt[0:1]


def _nsa_cmp_topk(qkv, kvc, vct, gates, ovt, tq, sub_tile):
    b, t, _ = qkv.shape
    g = NSA_GROUPS
    n_rows = kvc.shape[2]
    n_slc = t // NSA_SLC_LEN
    k_sel = min(NSA_SLC_TOPK, n_slc)
    nq = t // tq
    assert n_slc == LANES and (tq // sub_tile) * g == 8
    slopes = jnp.asarray(_alibi_slopes(NSA_HEADS))
    return pl.pallas_call(
        functools.partial(_nsa_cmp_topk_kernel, tq=tq, n_rows=n_rows, n_slc=n_slc, k_sel=k_sel,
                          sub_tile=sub_tile),
        out_shape=[
            jax.ShapeDtypeStruct((b, t, NSA_HEADS * NSA_HEAD_DIM), F32),
            jax.ShapeDtypeStruct((b, g, n_slc, t), BF16),
            jax.ShapeDtypeStruct((b, nq, 8, n_slc), F32),
        ],
        grid=(b, nq),
        in_specs=[
            pl.BlockSpec(memory_space=pltpu.SMEM),
            pl.BlockSpec((None, tq, NSA_HEADS * NSA_HEAD_DIM), lambda bb, i: (bb, i, 0)),
            pl.BlockSpec((None, g, n_rows, LANES), lambda bb, i: (bb, 0, 0, 0)),
            pl.BlockSpec((None, g, NSA_HEAD_DIM, n_rows), lambda bb, i: (bb, 0, 0, 0)),
            pl.BlockSpec((None, tq, g * LANES), lambda bb, i: (bb, i, 0)),
            pl.BlockSpec((n_slc, n_rows), lambda bb, i: (0, 0)),
        ],
        out_specs=[
            pl.BlockSpec((None, tq, NSA_HEADS * NSA_HEAD_DIM), lambda bb, i: (bb, i, 0)),
            pl.BlockSpec((None, g, n_slc, tq), lambda bb, i: (bb, 0, 0, i)),
            pl.BlockSpec((None, None, 8, n_slc), lambda bb, i: (bb, i, 0, 0)),
        ],
        scratch_shapes=[pltpu.VMEM((n_rows, NSA_HEADS * tq), F32)],
        compiler_params=_cparams(("arbitrary", "arbitrary")),
        name="nsa_cmp_topk",
    )(slopes, qkv, kvc, vct, gates, ovt)


def _nsa_sel_win_kernel(bits_ref, slope_ref, q_ref, kvs_ref, vst_ref, kvw_ref, vwt_ref, selt_ref, oc_ref,
                        gates_ref, eselt_ref, o_ref,
                        list_ref, kn_ref, sd_ref, wb_ref, s_ref, mc_ref, st_ref, acc_ref,
                        *, tq, tk, win_span, n_chunks):
    b = pl.program_id(0)
    g = pl.program_id(1)
    i = pl.program_id(2)
    q0 = i * tq
    hpg = NSA_HPG
    d = NSA_HEAD_DIM
    wide = hpg * tq
    lane = lax.broadcasted_iota(jnp.int32, (tq, LANES), 1)
    slopes = [slope_ref[g * hpg + hh] for hh in range(hpg)]

    def per_head_row(vals):
        return jnp.concatenate([jnp.full((1, tq), v, F32) for v in vals], axis=1)

    q4 = jnp.concatenate(
        [_head_query(q_ref[:, (hh // 2) * LANES:(hh // 2 + 1) * LANES], hh % 2, lane) for hh in range(hpg)],
        axis=0)

    rel = (lax.broadcasted_iota(jnp.int32, (tk, tq), 1)
           - lax.broadcasted_iota(jnp.int32, (tk, tq), 0))

    @pl.when(i == 0)
    def _():
        rel_f = rel.astype(F32)
        sd_ref[...] = jnp.concatenate([rel_f * slopes[hh] for hh in range(hpg)], axis=1)
        kf = kvs_ref[...].astype(F32)
        kf = jnp.where(lax.broadcasted_iota(jnp.int32, kf.shape, 1) < d, kf, 0.0)
        kn_ref[...] = jnp.broadcast_to(
            jnp.sqrt(jnp.max(jnp.sum(kf * kf, axis=1, keepdims=True), axis=0, keepdims=True)), kn_ref.shape)

    bits = bits_ref[(b * pl.num_programs(1) + g) * pl.num_programs(2) + i]
    q4f = q4.astype(F32)
    qn = jnp.max(jnp.sqrt(jnp.max(jnp.sum(q4f * q4f, axis=1, keepdims=True), axis=0, keepdims=True)))
    reach = (DIFF_ELIDE_LOG + DIFF_ELIDE_SLACK * 2.0 * qn * jnp.max(kn_ref[...])) * slope_ref[NSA_HEADS + g]

    def collect(j, n):
        near = (q0 - (j + 1) * tk + 1).astype(F32) < reach
        hit = jnp.where(near, lax.shift_right_logical(bits, j) & 1, 0)

        @pl.when(hit == 1)
        def _():
            list_ref[n] = j
        return n + hit

    n_act = lax.fori_loop(0, n_chunks, collect, 0)

    st_ref[0:1, :] = jnp.full((1, wide), M_FLOOR, F32)
    st_ref[1:2, :] = jnp.zeros((1, wide), F32)
    acc_ref[...] = jnp.zeros(acc_ref.shape, F32)
    selt = selt_ref[...]

    def scores(j):
        kv = kvs_ref[pl.ds(pl.multiple_of(j * tk, tk), tk), :]
        return _dot_nt(kv, q4)

    def produce(j, raw, slot):
        picked = _dot(eselt_ref[j], selt)
        causal = rel + (q0 - j * tk) >= 0
        madd = jnp.where((picked > 0.5) & causal, 0.0, NEG_INF)
        s = raw - sd_ref[...] + jnp.concatenate([madd] * hpg, axis=1)
        s_ref[slot] = s
        mc_ref[slot, 0:1, :] = jnp.max(s, axis=0, keepdims=True)

    def consume(j, slot):
        off = per_head_row([slopes[hh] * (q0 - j * tk).astype(F32) for hh in range(hpg)])
        m_prev = st_ref[0:1, :]
        l_prev = st_ref[1:2, :]
        m_next = jnp.maximum(m_prev, mc_ref[slot, 0:1, :] - off)
        alpha = jnp.exp(m_prev - m_next)
        p = jnp.exp(s_ref[slot] - (m_next + off))
        st_ref[0:1, :] = m_next
        st_ref[1:2, :] = alpha * l_prev + jnp.sum(p, axis=0, keepdims=True)
        vt = vst_ref[:, pl.ds(pl.multiple_of(j * tk, tk), tk)]
        acc_ref[...] = alpha * acc_ref[...] + _dot(vt, p.astype(BF16))

    j_first = list_ref[0]
    produce(j_first, scores(j_first), 0)

    def body(t, carry):
        j_new = list_ref[t]
        raw = scores(j_new)
        consume(list_ref[t - 1], (t - 1) & 1)
        produce(j_new, raw, t & 1)
        return carry

    lax.fori_loop(1, n_act, body, 0)
    consume(list_ref[n_act - 1], (n_act - 1) & 1)

    start = pl.multiple_of(jnp.maximum(q0 + tq - win_span, 0), tq)
    kvw = kvw_ref[pl.ds(start, win_span), :]
    s_w = _dot_nt(kvw, q4)

    @pl.when(i <= win_span // tq - 1)
    def _():
        dist_w = (lax.broadcasted_iota(jnp.int32, (win_span, tq), 1)
                  - lax.broadcasted_iota(jnp.int32, (win_span, tq), 0) + (q0 - start))
        valid_w = (dist_w >= 0) & (dist_w < NSA_WINDOW)
        dist_wf = dist_w.astype(F32)
        wb_ref[...] = jnp.concatenate(
            [jnp.where(valid_w, -slopes[hh] * dist_wf, NEG_INF) for hh in range(hpg)], axis=1)

    s_w = s_w + wb_ref[...]
    p_w = jnp.exp(s_w - jnp.max(s_w, axis=0, keepdims=True))
    l_w = jnp.sum(p_w, axis=0, keepdims=True)
    o_w = _dot(vwt_ref[:, pl.ds(start, win_span)], p_w.astype(BF16)) * (1.0 / l_w)

    l_s = st_ref[1:2, :]
    o_s = acc_ref[...] * jnp.where(l_s > 0.0, 1.0 / l_s, 0.0)
    gates_t = gates_ref[...].T
    g_s = jnp.concatenate([gates_t[3 * hh + 1:3 * hh + 2] for hh in range(hpg)], axis=1)
    g_w = jnp.concatenate([gates_t[3 * hh + 2:3 * hh + 3] for hh in range(hpg)], axis=1)
    comb = g_s * o_s + g_w * o_w
    by_head = jnp.concatenate([comb[:, hh * tq:(hh + 1) * tq] for hh in range(hpg)], axis=0)
    o_ref[...] = (by_head.T + oc_ref[...]).astype(o_ref.dtype)


def _nsa_sel_win(bits, qkv, vst, vwt, selt, oc, gates, eselt, tq, tk):
    b, t, _ = qkv.shape
    g = NSA_GROUPS
    d = NSA_HEAD_DIM
    n_slc = t // NSA_SLC_LEN
    nq = t // tq
    hd = NSA_HPG * d
    wide = NSA_HPG * tq
    kv_col0 = NSA_HEADS * d // LANES
    win_span = min(NSA_WINDOW + tq, t)
    n_chunks = t // tk
    head_slopes = _alibi_slopes(NSA_HEADS)
    slopes = jnp.asarray(np.concatenate([head_slopes, 1.0 / head_slopes.reshape(g, NSA_HPG).min(axis=1)]))
    grid_spec = pltpu.PrefetchScalarGridSpec(
        num_scalar_prefetch=1,
        grid=(b, g, nq),
        in_specs=[
            pl.BlockSpec(memory_space=pltpu.SMEM),
            pl.BlockSpec((None, tq, hd), lambda bb, gg, i, bits: (bb, i, gg)),
            pl.BlockSpec((None, t, LANES), lambda bb, gg, i, bits: (bb, 0, kv_col0 + gg)),
            pl.BlockSpec((None, d, t), lambda bb, gg, i, bits: (bb, gg, 0)),
            pl.BlockSpec((None, t, LANES), lambda bb, gg, i, bits: (bb, 0, kv_col0 + g + gg)),
            pl.BlockSpec((None, d, t), lambda bb, gg, i, bits: (bb, gg, 0)),
            pl.BlockSpec((None, None, n_slc, tq), lambda bb, gg, i, bits: (bb, gg, 0, i)),
            pl.BlockSpec((None, tq, hd), lambda bb, gg, i, bits: (bb, i, gg)),
            pl.BlockSpec((None, tq, LANES), lambda bb, gg, i, bits: (bb, i, gg)),
            pl.BlockSpec(eselt.shape, lambda bb, gg, i, bits: (0, 0, 0)),
        ],
        out_specs=pl.BlockSpec((None, tq, hd), lambda bb, gg, i, bits: (bb, i, gg)),
        scratch_shapes=[
            pltpu.SMEM((n_chunks,), jnp.int32),
            pltpu.VMEM((8, LANES), F32),
            pltpu.VMEM((tk, wide), F32),
            pltpu.VMEM((win_span, wide), F32),
            pltpu.VMEM((2, tk, wide), F32),
            pltpu.VMEM((2, 8, wide), F32),
            pltpu.VMEM((8, wide), F32),
            pltpu.VMEM((d, wide), F32),
        ],
    )
    return pl.pallas_call(
        functools.partial(_nsa_sel_win_kernel, tq=tq, tk=tk, win_span=win_span, n_chunks=n_chunks),
        out_shape=jax.ShapeDtypeStruct((b, t, NSA_HEADS * d), BF16),
        grid_spec=grid_spec,
        compiler_params=_cparams(("arbitrary", "arbitrary", "arbitrary")),
        name="nsa_sel_win",
    )(bits, slopes, qkv, qkv, vst, qkv, vwt, selt, oc, gates, eselt)


def _nsa_weight_layout():
    hd, g, d = NSA_HEADS * NSA_HEAD_DIM, NSA_GROUPS, NSA_HEAD_DIM
    base = {name: hd + n * g * d for n, name in enumerate(["kc", "vc", "ks", "vs", "kw", "vw"])}
    gate0 = hd + 6 * g * d
    cols = list(range(hd))
    for kn, vn in (("ks", "vs"), ("kw", "vw"), ("kc", "vc")):
        for gg in range(g):
            cols += list(range(base[kn] + gg * d, base[kn] + (gg + 1) * d))
            cols += list(range(base[vn] + gg * d, base[vn] + (gg + 1) * d))
    for gg in range(g):
        n_gate = NSA_HPG * 3
        cols += list(range(gate0 + gg * n_gate, gate0 + (gg + 1) * n_gate)) + [-1] * (LANES - n_gate)
    return np.asarray(cols, np.int32)


def _permute_columns(w, cols):
    picked = jnp.take(w, jnp.asarray(np.maximum(cols, 0)), axis=1)
    return jnp.where(jnp.asarray(cols >= 0)[None, :], picked, 0.0)


def _overlap_t(n_rows, n_slc):
    n_cmp = n_rows - 1
    c_start = np.arange(n_rows) * NSA_CMP_STRIDE
    s_start = np.arange(n_slc) * NSA_SLC_LEN
    ov = (np.minimum(c_start[:, None] + NSA_CMP_LEN, s_start[None, :] + NSA_SLC_LEN)
          - np.maximum(c_start[:, None], s_start[None, :]))
    ov = np.clip(ov, 0, None).astype(np.float32) / NSA_CMP_LEN
    ov[n_cmp:] = 0.0
    return jnp.asarray(ov.T, BF16)


def _block_expander(n_slc, tk):
    n_chunks = n_slc * NSA_SLC_LEN // tk
    key_blk = (np.arange(n_chunks)[:, None] * tk + np.arange(tk)[None, :]) // NSA_SLC_LEN
    e = (key_blk[:, :, None] == np.arange(n_slc)[None, None, :]).astype(np.float32)
    return jnp.asarray(e, BF16)


def _chunk_bits(any_sel, sub_per_tile, blocks_per_chunk):
    b, nq, _, n_slc = any_sel.shape
    g = NSA_GROUPS
    n_chunks = n_slc // blocks_per_chunk
    a = any_sel.reshape(b, nq, sub_per_tile, g, n_chunks, blocks_per_chunk).max(axis=-1) > 0.5
    a = a.transpose(0, 3, 1, 2, 4).reshape(b * g * nq * sub_per_tile, n_chunks)
    weights = jnp.left_shift(jnp.uint32(1), jnp.arange(n_chunks, dtype=jnp.uint32))
    return lax.bitcast_convert_type(jnp.sum(jnp.where(a, weights[None, :], jnp.uint32(0)), axis=1,
                                            dtype=jnp.uint32), jnp.int32)


def _diff_mixer(x, mod, layer, w_in, w_out, lam_q1, lam_k1, lam_q2, lam_k2, subln_g):
    n_qk = 2 * DIFF_HEADS * 2 * DIFF_HEAD_DIM
    n_v = DIFF_HEADS * DIFF_V_DIM
    w = w_in.astype(BF16)
    qk, vt = _modulated_projection(x, mod, layer, w[:, :n_qk], [(0, n_qk, "id")], [BF16],
                                   w[:, n_qk:].T, [(0, n_v)])
    lam_vecs = jnp.stack([lam_q1, lam_k1, lam_q2, lam_k2]).astype(F32)
    o = _diff_attention(qk, vt, lam_vecs, subln_g.astype(F32), layer)
    return o, w_out.astype(BF16)


def _nsa_mixer(x, mod, layer, w_in, pos_k, pos_v, k_w1, k_w2, v_w1, v_w2, w_out):
    b, t, _ = x.shape
    g, d = NSA_GROUPS, NSA_HEAD_DIM
    hd = NSA_HEADS * d
    cols = _nsa_weight_layout()
    w = _permute_columns(w_in, cols).astype(BF16)
    n_main = hd + 2 * (2 * g * d)
    n_cmp_cols = 2 * g * d
    plan = [(0, n_main, "id"), (n_main, n_main + n_cmp_cols, "id"),
            (n_main + n_cmp_cols, n_main + n_cmp_cols + g * LANES, "sigmoid")]
    v_cols = np.concatenate([np.arange(hd + 6 * g * d)[hd + (2 * n + 1) * g * d: hd + (2 * n + 2) * g * d]
                             for n in (1, 2)])
    wt = jnp.take(w_in, jnp.asarray(v_cols), axis=1).T.astype(BF16)
    qkv, kvc_tok, gates, vst, vwt = _modulated_projection(
        x, mod, layer, w, plan, [BF16, F32, F32], wt, [(0, g * d), (g * d, 2 * g * d)])

    rows = t // NSA_CMP_STRIDE
    pos = jnp.stack([pos_k.reshape(1, -1), pos_v.reshape(1, -1)]).astype(F32)
    w1 = jnp.stack([k_w1, v_w1]).astype(BF16)
    zeros = jnp.zeros((NSA_CMP_HIDDEN, d), F32)
    w2pad = jnp.stack([jnp.concatenate([k_w2, zeros], axis=1),
                       jnp.concatenate([zeros, v_w2], axis=1)]).astype(BF16)
    kvc, vct = _compress(kvc_tok, pos, w1, w2pad)

    n_slc = t // NSA_SLC_LEN
    tq_sel = min(SEL_Q_TILE, t)
    tq_cmp = min(CMP_Q_TILE, t)
    tk = min(SEL_K_TILE, t)
    oc, selt, any_sel = _nsa_cmp_topk(qkv, kvc, vct, gates, _overlap_t(rows, n_slc), tq_cmp, tq_sel)
    bits = _chunk_bits(any_sel, tq_cmp // tq_sel, tk // NSA_SLC_LEN)
    o = _nsa_sel_win(bits, qkv, vst, vwt, selt, oc, gates, _block_expander(n_slc, tk), tq_sel, tk)
    return o, w_out.astype(BF16)


def kernel(x, c, diff_w_in, diff_w_out, diff_lam_q1, diff_lam_k1, diff_lam_q2, diff_lam_k2, diff_subln_g,
           nsa_w_in, nsa_cmp_pos_k, nsa_cmp_pos_v, nsa_cmp_k_w1, nsa_cmp_k_w2, nsa_cmp_v_w1, nsa_cmp_v_w2,
           nsa_w_out, ada_w, ada_b, ln_mix_g, ln_mix_b, ln_ffn_g, ln_ffn_b, ffn_w_gu, ffn_w_down):
    mod = _ada_modulation(c, ada_w, ada_b)
    for i in range(DEPTH):
        j = i // N_MIXERS
        if i % N_MIXERS == 0:
            o, w_out = _diff_mixer(x, mod, i, diff_w_in[j], diff_w_out[j], diff_lam_q1[j], diff_lam_k1[j],
                                   diff_lam_q2[j], diff_lam_k2[j], diff_subln_g[j])
        else:
            o, w_out = _nsa_mixer(x, mod, i, nsa_w_in[j], nsa_cmp_pos_k[j], nsa_cmp_pos_v[j],
                                  nsa_cmp_k_w1[j], nsa_cmp_k_w2[j], nsa_cmp_v_w1[j], nsa_cmp_v_w2[j],
                                  nsa_w_out[j])
        x = _mix_ffn(o, w_out, x, mod, i, ln_mix_g[i], ln_mix_b[i],
                     ffn_w_gu[i].astype(BF16), ffn_w_down[i].astype(BF16), ln_ffn_g[i], ln_ffn_b[i])
    return x
```

```python
import functools
import math

import numpy as np
import jax
import jax.numpy as jnp
from jax import lax
from jax.experimental import pallas as pl
from jax.experimental.pallas import tpu as pltpu

F32 = jnp.float32
BF16 = jnp.bfloat16

D_MODEL = 1024
DEPTH = 2
N_MIXERS = 2
DEEPNORM_ALPHA = (2 * DEPTH) ** 0.25
LN_EPS = 1e-5
RMS_EPS = 1e-5
NEG_INF = -1e30
FORCED_SCORE = 1e6

DIFF_HEADS = 8
DIFF_HEAD_DIM = 64
DIFF_V_DIM = 2 * DIFF_HEAD_DIM

NSA_HEADS = 16
NSA_GROUPS = 4
NSA_HEAD_DIM = 64
NSA_HPG = NSA_HEADS // NSA_GROUPS
NSA_CMP_LEN = 32
NSA_CMP_STRIDE = 16
NSA_CMP_HIDDEN = 128
NSA_SLC_LEN = 64
NSA_SLC_TOPK = 16
NSA_WINDOW = 512

FFN_HIDDEN = -(-8 * D_MODEL // (3 * 256)) * 256

LANES = 128
VMEM_LIMIT = 56 * 1024 * 1024
PROJ_ROWS = 512
FFN_CHUNK = FFN_HIDDEN // 2
DIFF_TILE = 256
DIFF_STREAMS = 2
DIFF_ONES_ROWS = 16
DIFF_ELIDE_LOG = 110.0
DIFF_ELIDE_SLACK = 1.02
CMP_Q_TILE = 256
SEL_Q_TILE = 128
SEL_K_TILE = 256
SEL_STREAMS = 2
M_FLOOR = -1e29
TAKEN = -3e38


def _cparams(sem):
    return pltpu.CompilerParams(dimension_semantics=sem, vmem_limit_bytes=VMEM_LIMIT)


def _dot(a, b):
    return jnp.dot(a, b, preferred_element_type=F32)


def _dot_nt(a, b):
    return lax.dot_general(a, b, (((1,), (1,)), ((), ())), preferred_element_type=F32)


def _layer_norm(z, g, b):
    mu = jnp.mean(z, axis=-1, keepdims=True)
    zc = z - mu
    var = jnp.mean(zc * zc, axis=-1, keepdims=True)
    return zc * lax.rsqrt(var + LN_EPS) * g + b


def _alibi_slopes(n_heads):
    return (2.0 ** (-8.0 * np.arange(1, n_heads + 1, dtype=np.float32) / n_heads)).astype(np.float32)


def _ada_kernel(c_ref, w_ref, b_ref, o_ref):
    c = c_ref[...]
    s = (c * jax.nn.sigmoid(c)).astype(BF16)
    o_ref[0] = _dot(s, w_ref[0].astype(BF16)) + b_ref[0]


def _ada_modulation(c, ada_w, ada_b):
    depth, d, d6 = ada_w.shape
    b = c.shape[0]
    rows = 8
    c_pad = jnp.zeros((rows, d), F32).at[:b].set(c)
    out = pl.pallas_call(
        _ada_kernel,
        out_shape=jax.ShapeDtypeStruct((depth, rows, d6), F32),
        grid=(depth, d6 // d),
        in_specs=[
            pl.BlockSpec((rows, d), lambda l, j: (0, 0)),
            pl.BlockSpec((1, d, d), lambda l, j: (l, 0, j)),
            pl.BlockSpec((1, 1, d), lambda l, j: (l, 0, j)),
        ],
        out_specs=pl.BlockSpec((1, rows, d), lambda l, j: (l, 0, j)),
        compiler_params=_cparams(("arbitrary", "arbitrary")),
        name="ada_modulation",
    )(c_pad, ada_w, ada_b.reshape(depth, 1, d6))
    return out[:, :b].reshape(depth, b, 6, 1, d)


def _mod_spec(layer, which, n_grid):
    d = D_MODEL
    if n_grid == 2:
        return pl.BlockSpec((None, None, None, 1, d), lambda b, i: (layer, b, which, 0, 0))
    return pl.BlockSpec((None, None, None, 1, d), lambda b, i, k: (layer, b, which, 0, 0))


def _proj_kernel(x_ref, sc_ref, sh_ref, w_ref, wt_ref, *out_refs, plan, t_plan):
    h = (x_ref[...] * (1.0 + sc_ref[...]) + sh_ref[...]).astype(BF16)
    for ref, (c0, c1, kind), in zip(out_refs, plan):
        step = 1024
        for s0 in range(c0, c1, step):
            s1 = min(s0 + step, c1)
            y = _dot(h, w_ref[:, s0:s1])
            if kind == "sigmoid":
                y = jax.nn.sigmoid(y)
            ref[:, s0 - c0:s1 - c0] = y.astype(ref.dtype)
    for ref, (r0, r1) in zip(out_refs[len(plan):], t_plan):
        ref[...] = _dot_nt(wt_ref[r0:r1, :], h).astype(ref.dtype)


def _modulated_projection(x, mod, layer, w, plan, out_dtypes, wt, t_plan):
    b, t, d = x.shape
    n = w.shape[1]
    tm = min(PROJ_ROWS, t)
    out_shape = [jax.ShapeDtypeStruct((b, t, c1 - c0), dt) for (c0, c1, _), dt in zip(plan, out_dtypes)]
    out_specs = [pl.BlockSpec((None, tm, c1 - c0), lambda bb, i: (bb, i, 0)) for (c0, c1, _) in plan]
    out_shape += [jax.ShapeDtypeStruct((b, r1 - r0, t), BF16) for (r0, r1) in t_plan]
    out_specs += [pl.BlockSpec((None, r1 - r0, tm), lambda bb, i: (bb, 0, i)) for (r0, r1) in t_plan]
    return pl.pallas_call(
        functools.partial(_proj_kernel, plan=plan, t_plan=t_plan),
        out_shape=out_shape,
        grid=(b, t // tm),
        in_specs=[
            pl.BlockSpec((None, tm, d), lambda bb, i: (bb, i, 0)),
            _mod_spec(layer, 1, 2),
            _mod_spec(layer, 0, 2),
            pl.BlockSpec((d, n), lambda bb, i: (0, 0)),
            pl.BlockSpec(wt.shape, lambda bb, i: (0, 0)),
        ],
        out_specs=out_specs,
        compiler_params=_cparams(("arbitrary", "arbitrary")),
        name=f"mod_proj_l{layer}",
    )(x, mod, mod, w, wt)


def _diff_attn_kernel(slope_ref, lam_ref, g_ref, q_ref, k_ref, vt_ref, o_ref, kn_ref, *scratch,
                      tile, lam_init, n_streams):
    h = pl.program_id(1)
    qi = pl.program_id(2)
    slope = slope_ref[0, h]
    inv_slope = slope_ref[1, h]
    half = DIFF_HEAD_DIM
    s_refs, mc_refs, st_refs, acc_refs = (scratch[n::4] for n in range(4))

    key = lax.broadcasted_iota(jnp.int32, (tile, tile), 0)
    qry = lax.broadcasted_iota(jnp.int32, (tile, tile), 1)
    lane = lax.broadcasted_iota(jnp.int32, (tile, LANES), 1)
    pos = lax.broadcasted_iota(jnp.int32, (tile, LANES), 0).astype(F32) * slope
    k_extra = jnp.where(lane == 0, 1.0, jnp.where(lane == 1, pos, 0.0)).astype(BF16)
    q_extra = jnp.where(lane == 0, -pos, jnp.where(lane == 1, 1.0, 0.0)).astype(BF16)
    ones_rows = jnp.ones((DIFF_ONES_ROWS, tile), BF16)

    @pl.when(qi == 0)
    def _():
        for n in range(n_streams):
            kf = k_ref[n].astype(F32)
            kn_ref[n] = jnp.max(jnp.sqrt(jnp.max(jnp.sum(kf * kf, axis=1, keepdims=True),
                                                 axis=0, keepdims=True)))

    qcs = []
    reach_max = jnp.float32(0.0)
    for n in range(n_streams):
        qf = q_ref[n].astype(F32) * (DIFF_HEAD_DIM ** -0.5)
        qs = qf.astype(BF16)
        zero = jnp.zeros_like(qs)
        qcs.append((jnp.concatenate([jnp.where(lane < half, qs, zero), q_extra], axis=1),
                    jnp.concatenate([jnp.where(lane >= half, qs, zero), q_extra], axis=1)))
        qn = jnp.max(jnp.sqrt(jnp.max(jnp.sum(qf * qf, axis=1, keepdims=True), axis=0, keepdims=True)))
        reach = (DIFF_ELIDE_LOG + DIFF_ELIDE_SLACK * 2.0 * qn * kn_ref[n]) * inv_slope
        reach_max = jnp.maximum(reach_max, reach)
        acc_refs[n][...] = jnp.zeros(acc_refs[n].shape, F32)
        st_refs[n][...] = jnp.full(st_refs[n].shape, M_FLOOR, F32)
    n_keep = lax.fori_loop(
        0, qi, lambda a, cnt: cnt + jnp.where((a * tile + 1).astype(F32) < reach_max, 1, 0), 0)
    j0 = qi - n_keep

    def scores(n, j):
        k = k_ref[n, pl.ds(pl.multiple_of(j * tile, tile), tile), :]
        k = jnp.concatenate([k, k_extra], axis=1)
        return [_dot_nt(k, qcs[n][c]) for c in range(2)]

    def produce(n, j, raw, diagonal):
        slot = j & 1
        for c in range(2):
            s = raw[c]
            if diagonal is True:
                s = jnp.where(key <= qry, s, NEG_INF)
            elif diagonal is not False:
                s = jnp.where((key <= qry) | jnp.logical_not(diagonal), s, NEG_INF)
            s_refs[n][slot, c] = s
            mc_refs[n][slot, c:c + 1, :] = jnp.max(s, axis=0, keepdims=True)

    def consume(n, j):
        slot = j & 1
        vt = vt_ref[n, :, pl.ds(pl.multiple_of(j * tile, tile), tile)]
        vt = jnp.concatenate([vt, ones_rows], axis=0)
        off = slope * ((qi - j) * tile).astype(F32)
        st_ref, acc_ref = st_refs[n], acc_refs[n]
        for c in range(2):
            m_prev = st_ref[c:c + 1, :]
            m_next = jnp.maximum(m_prev, mc_refs[n][slot, c:c + 1, :] - off)
            alpha = jnp.exp(m_prev - m_next)
            p = jnp.exp(s_refs[n][slot, c] - (m_next + off))
            st_ref[c:c + 1, :] = m_next
            acc_ref[c] = alpha * acc_ref[c] + _dot(vt, p.astype(BF16))

    for n in range(n_streams):
        produce(n, j0, scores(n, j0), j0 == qi)

    def step(t, diagonal):
        raws = [scores(n, t) for n in range(n_streams)]
        for n in range(n_streams):
            consume(n, t - 1)
            produce(n, t, raws[n], diagonal)

    def body(t, carry):
        step(t, False)
        return carry

    lax.fori_loop(j0 + 1, qi, body, 0)

    @pl.when(qi > j0)
    def _():
        step(qi, True)

    lam_v = lam_ref[...]
    lam = (jnp.exp(jnp.sum(lam_v[0:1] * lam_v[1:2], axis=1, keepdims=True))
           - jnp.exp(jnp.sum(lam_v[2:3] * lam_v[3:4], axis=1, keepdims=True)) + lam_init)
    for n in range(n_streams):
        consume(n, qi)
        dv = DIFF_V_DIM
        l0 = acc_refs[n][0, dv:dv + 1, :]
        l1 = acc_refs[n][1, dv:dv + 1, :]
        o = acc_refs[n][0, :dv, :] / l0 - lam * (acc_refs[n][1, :dv, :] / l1)
        o = o * lax.rsqrt(jnp.mean(o * o, axis=0, keepdims=True) + RMS_EPS)
        o_ref[n] = ((o * g_ref[...] * (1.0 - lam_init)).T).astype(o_ref.dtype)


def _diff_attention(qk, vt, lam_vecs, subln_g, layer_idx):
    b, t, _ = qk.shape
    hh = DIFF_HEADS
    tile = min(DIFF_TILE, t)
    ns = DIFF_STREAMS if b % DIFF_STREAMS == 0 else 1
    lam_init = 0.8 - 0.6 * math.exp(-0.3 * layer_idx)
    slopes = _alibi_slopes(hh)
    slope_tab = jnp.asarray(np.stack([slopes, 1.0 / slopes]))
    per_stream = [
        pltpu.VMEM((2, 2, tile, tile), F32),
        pltpu.VMEM((2, 8, tile), F32),
        pltpu.VMEM((8, tile), F32),
        pltpu.VMEM((2, DIFF_V_DIM + DIFF_ONES_ROWS, tile), F32),
    ]
    col = (slopes[:, None] * np.arange(tile, dtype=np.float32)[None, :]).astype(np.float32)
    assert tile <= 256 and np.array_equal(col.astype(jnp.bfloat16).astype(np.float32), col)
    return pl.pallas_call(
        functools.partial(_diff_attn_kernel, tile=tile, lam_init=lam_init, n_streams=ns),
        out_shape=jax.ShapeDtypeStruct((b, t, hh * DIFF_V_DIM), BF16),
        grid=(b // ns, hh, t // tile),
        in_specs=[
            pl.BlockSpec(memory_space=pltpu.SMEM),
            pl.BlockSpec((4, DIFF_HEAD_DIM), lambda bb, h, i: (0, 0)),
            pl.BlockSpec((DIFF_V_DIM, 1), lambda bb, h, i: (0, 0)),
            pl.BlockSpec((ns, tile, LANES), lambda bb, h, i: (bb, i, h)),
            pl.BlockSpec((ns, t, LANES), lambda bb, h, i: (bb, 0, hh + h)),
            pl.BlockSpec((ns, DIFF_V_DIM, t), lambda bb, h, i: (bb, h, 0)),
        ],
        out_specs=pl.BlockSpec((ns, tile, LANES), lambda bb, h, i: (bb, i, h)),
        scratch_shapes=[pltpu.SMEM((ns,), F32)] + per_stream * ns,
        compiler_params=_cparams(("arbitrary", "arbitrary", "arbitrary")),
        name="diff_attention",
    )(slope_tab, lam_vecs, subln_g.reshape(DIFF_V_DIM, 1), qk, qk, vt)


def _mix_ffn_kernel(o_ref, wo_ref, x_ref, ga_ref, lng_a_ref, lnb_a_ref, sc_ref, sh_ref, gf_ref,
                    wg_ref, wu_ref, wd_ref, lng_f_ref, lnb_f_ref, out_ref, x1_ref, h_ref, acc_ref):
    k = pl.program_id(2)

    @pl.when(k == 0)
    def _():
        y = _dot(o_ref[...], wo_ref[...])
        z = DEEPNORM_ALPHA * x_ref[...] + (1.0 + ga_ref[...]) * y
        x1 = _layer_norm(z, lng_a_ref[...], lnb_a_ref[...])
        x1_ref[...] = x1
        h_ref[...] = (x1 * (1.0 + sc_ref[...]) + sh_ref[...]).astype(BF16)
        acc_ref[...] = jnp.zeros(acc_ref.shape, F32)

    h = h_ref[...]
    gp = _dot(h, wg_ref[...])
    up = _dot(h, wu_ref[...])
    act = (gp * jax.nn.sigmoid(gp) * up).astype(BF16)
    acc_ref[...] += _dot(act, wd_ref[...])

    @pl.when(k == pl.num_programs(2) - 1)
    def _():
        z = DEEPNORM_ALPHA * x1_ref[...] + (1.0 + gf_ref[...]) * acc_ref[...]
        out_ref[...] = _layer_norm(z, lng_f_ref[...], lnb_f_ref[...])


def _mix_ffn(o, w_out, x, mod, layer, ln_mix_g, ln_mix_b, w_gu, w_down, ln_ffn_g, ln_ffn_b):
    b, t, d = x.shape
    dh = o.shape[-1]
    fh = w_down.shape[0]
    th = min(FFN_CHUNK, fh)
    nh = fh // th
    tm = min(PROJ_ROWS, t)
    row_vec = pl.BlockSpec((1, d), lambda bb, i, k: (0, 0))
    return pl.pallas_call(
        _mix_ffn_kernel,
        out_shape=jax.ShapeDtypeStruct((b, t, d), F32),
        grid=(b, t // tm, nh),
        in_specs=[
            pl.BlockSpec((None, tm, dh), lambda bb, i, k: (bb, i, 0)),
            pl.BlockSpec((dh, d), lambda bb, i, k: (0, 0)),
            pl.BlockSpec((None, tm, d), lambda bb, i, k: (bb, i, 0)),
            _mod_spec(layer, 2, 3),
            row_vec,
            row_vec,
            _mod_spec(layer, 4, 3),
            _mod_spec(layer, 3, 3),
            _mod_spec(layer, 5, 3),
            pl.BlockSpec((d, th), lambda bb, i, k: (0, k)),
            pl.BlockSpec((d, th), lambda bb, i, k: (0, nh + k)),
            pl.BlockSpec((th, d), lambda bb, i, k: (k, 0)),
            row_vec,
            row_vec,
        ],
        out_specs=pl.BlockSpec((None, tm, d), lambda bb, i, k: (bb, i, 0)),
        scratch_shapes=[pltpu.VMEM((tm, d), F32), pltpu.VMEM((tm, d), BF16), pltpu.VMEM((tm, d), F32)],
        compiler_params=_cparams(("arbitrary", "arbitrary", "arbitrary")),
        name=f"mix_ffn_l{layer}",
    )(o, w_out, x, mod, ln_mix_g.reshape(1, d), ln_mix_b.reshape(1, d), mod, mod, mod,
      w_gu, w_gu, w_down, ln_ffn_g.reshape(1, d), ln_ffn_b.reshape(1, d))


def _gelu_tanh(x):
    return 0.5 * x * (1.0 + jnp.tanh(math.sqrt(2.0 / math.pi) * (x + 0.044715 * (x * x * x))))


def _compress_kernel(x_ref, pos_ref, w1_ref, w2_ref, o_ref, vt_ref):
    d = NSA_HEAD_DIM
    n_rows, stride, _ = x_ref.shape
    half = stride * d
    lane = lax.broadcasted_iota(jnp.int32, (n_rows, LANES), 1)
    pieces = ([], [])
    for lp in range(stride // 2):
        even = x_ref[:, 2 * lp, :]
        odd = x_ref[:, 2 * lp + 1, :]
        pieces[0].append(jnp.where(lane < d, even, pltpu.roll(odd, d, 1)))
        pieces[1].append(jnp.where(lane < d, pltpu.roll(even, d, 1), odd))
    out = None
    for kv in range(2):
        r = jnp.concatenate(pieces[kv], axis=1)
        r_next = pltpu.roll(r, n_rows - 1, 0)
        a = (r + pos_ref[kv, :, :half]).astype(BF16)
        bnx = (r_next + pos_ref[kv, :, half:]).astype(BF16)
        hid = _dot(a, w1_ref[kv, :half, :]) + _dot(bnx, w1_ref[kv, half:, :])
        y = _dot(_gelu_tanh(hid).astype(BF16), w2_ref[kv])
        out = y if out is None else out + y
    o_ref[...] = out.astype(o_ref.dtype)
    vt_ref[...] = out.T[NSA_HEAD_DIM:, :].astype(vt_ref.dtype)


def _compress(kvc_tok, pos, w1, w2pad):
    b, t, _ = kvc_tok.shape
    g = NSA_GROUPS
    rows = t // NSA_CMP_STRIDE
    width = NSA_CMP_STRIDE * NSA_HEAD_DIM
    return pl.pallas_call(
        _compress_kernel,
        out_shape=[jax.ShapeDtypeStruct((b, g, rows, LANES), BF16),
                   jax.ShapeDtypeStruct((b, g, NSA_HEAD_DIM, rows), BF16)],
        grid=(b, g),
        in_specs=[
            pl.BlockSpec((None, rows, NSA_CMP_STRIDE, LANES), lambda bb, gg: (bb, 0, 0, gg)),
            pl.BlockSpec((2, 1, 2 * width), lambda bb, gg: (0, 0, 0)),
            pl.BlockSpec((2, 2 * width, NSA_CMP_HIDDEN), lambda bb, gg: (0, 0, 0)),
            pl.BlockSpec((2, NSA_CMP_HIDDEN, LANES), lambda bb, gg: (0, 0, 0)),
        ],
        out_specs=[pl.BlockSpec((None, None, rows, LANES), lambda bb, gg: (bb, gg, 0, 0)),
                   pl.BlockSpec((None, None, NSA_HEAD_DIM, rows), lambda bb, gg: (bb, gg, 0, 0))],
        compiler_params=_cparams(("arbitrary", "arbitrary")),
        name="nsa_compress",
    )(kvc_tok.reshape(b, rows, NSA_CMP_STRIDE, g * LANES), pos, w1, w2pad)


def _head_query(q_pair, head_in_pair, lane):
    d = NSA_HEAD_DIM
    qf = q_pair.astype(F32) * (d ** -0.5)
    if head_in_pair == 1:
        qf = pltpu.roll(qf, d, 1)
    return jnp.where(lane < d, qf, 0.0).astype(BF16)


def _merge_head_pair(o_even, o_odd, lane):
    return jnp.where(lane < NSA_HEAD_DIM, pltpu.roll(o_even, NSA_HEAD_DIM, 1), o_odd)


def _nsa_cmp_topk_kernel(slope_ref, q_ref, kvc_ref, vct_ref, gates_ref, ovt_ref,
                         oc_ref, sel_ref, any_ref, sdc_ref, *, tq, n_rows, n_slc, k_sel, sub_tile):
    i = pl.program_id(1)
    q0 = i * tq
    hpg = NSA_HPG
    d = NSA_HEAD_DIM
    lane = lax.broadcasted_iota(jnp.int32, (tq, LANES), 1)
    qry_c = lax.broadcasted_iota(jnp.int32, (n_rows, tq), 1)
    end_c = lax.broadcasted_iota(jnp.int32, (n_rows, tq), 0) * NSA_CMP_STRIDE + (NSA_CMP_LEN - 1)

    @pl.when(i == 0)
    def _():
        rel_c = (qry_c - end_c).astype(F32)
        for h in range(NSA_HEADS):
            sdc_ref[:, h * tq:(h + 1) * tq] = rel_c * slope_ref[h]

    valid = end_c <= q0 + qry_c

    blk_i = lax.broadcasted_iota(jnp.int32, (n_slc, tq), 0)
    qpos = q0 + lax.broadcasted_iota(jnp.int32, (n_slc, tq), 1)
    q_blk = lax.shift_right_logical(qpos, int(math.log2(NSA_SLC_LEN)))
    forced = (blk_i == 0) | (blk_i == q_blk) | (blk_i == q_blk - 1)
    causal_blk = blk_i * NSA_SLC_LEN <= qpos
    blk_f = blk_i.astype(F32)

    for g in range(NSA_GROUPS):
        q4 = jnp.concatenate(
            [_head_query(q_ref[:, ((g * hpg + hh) // 2) * LANES:((g * hpg + hh) // 2 + 1) * LANES], hh % 2, lane)
             for hh in range(hpg)], axis=0)
        raw = _dot_nt(kvc_ref[g], q4)
        s = jnp.concatenate(
            [jnp.where(valid, raw[:, hh * tq:(hh + 1) * tq]
                       - sdc_ref[:, (g * hpg + hh) * tq:(g * hpg + hh + 1) * tq], NEG_INF)
             for hh in range(hpg)], axis=1)
        m = jnp.max(s, axis=0, keepdims=True)
        p = jnp.exp(s - m)
        l = jnp.sum(p, axis=0, keepdims=True)
        inv = jnp.where(m > 0.5 * NEG_INF, 1.0 / l, 0.0)
        lhs = jnp.concatenate([vct_ref[g], ovt_ref[...]], axis=0)
        r = _dot(lhs, p.astype(BF16))
        gates_t = gates_ref[:, g * LANES:(g + 1) * LANES].T
        w = inv * jnp.concatenate([gates_t[3 * hh:3 * hh + 1] for hh in range(hpg)], axis=1)
        oc_t = r[:d] * w
        by_head = jnp.concatenate([oc_t[:, hh * tq:(hh + 1) * tq] for hh in range(hpg)], axis=0)
        oc_ref[:, g * hpg * d:(g + 1) * hpg * d] = by_head.T
        imp_t = r[d:, 0:tq] * inv[:, 0:tq]
        for hh in range(1, hpg):
            imp_t = imp_t + r[d:, hh * tq:(hh + 1) * tq] * inv[:, hh * tq:(hh + 1) * tq]

        imp = jnp.where(causal_blk, jnp.where(forced, FORCED_SCORE, imp_t), NEG_INF)
        for _ in range(k_sel):
            mx = jnp.max(imp, axis=0, keepdims=True)
            cand = jnp.where(imp == mx, blk_f, float(n_slc))
            pick = blk_f == jnp.min(cand, axis=0, keepdims=True)
            imp = jnp.where(pick, TAKEN, imp)
        sel_b = jnp.where(causal_blk & (imp < 0.5 * TAKEN), 1.0, 0.0).astype(BF16)
        sel_ref[g] = sel_b
        ones = jnp.ones((8, sub_tile), BF16)
        for sub in range(tq // sub_tile):
            r = sub * NSA_GROUPS + g
            cnt = _dot_nt(ones, sel_b[:, sub * sub_tile:(sub + 1) * sub_tile])
            any_ref[r:r + 1, :] = cnt[0:1]


def _nsa_cmp_topk(qkv, kvc, vct, gates, ovt, tq, sub_tile):
    b, t, _ = qkv.shape
    g = NSA_GROUPS
    n_rows = kvc.shape[2]
    n_slc = t // NSA_SLC_LEN
    k_sel = min(NSA_SLC_TOPK, n_slc)
    nq = t // tq
    assert n_slc == LANES and (tq // sub_tile) * g == 8
    slopes = jnp.asarray(_alibi_slopes(NSA_HEADS))
    return pl.pallas_call(
        functools.partial(_nsa_cmp_topk_kernel, tq=tq, n_rows=n_rows, n_slc=n_slc, k_sel=k_sel,
                          sub_tile=sub_tile),
        out_shape=[
            jax.ShapeDtypeStruct((b, t, NSA_HEADS * NSA_HEAD_DIM), F32),
            jax.ShapeDtypeStruct((b, g, n_slc, t), BF16),
            jax.ShapeDtypeStruct((b, nq, 8, n_slc), F32),
        ],
        grid=(b, nq),
        in_specs=[
            pl.BlockSpec(memory_space=pltpu.SMEM),
            pl.BlockSpec((None, tq, NSA_HEADS * NSA_HEAD_DIM), lambda bb, i: (bb, i, 0)),
            pl.BlockSpec((None, g, n_rows, LANES), lambda bb, i: (bb, 0, 0, 0)),
            pl.BlockSpec((None, g, NSA_HEAD_DIM, n_rows), lambda bb, i: (bb, 0, 0, 0)),
            pl.BlockSpec((None, tq, g * LANES), lambda bb, i: (bb, i, 0)),
            pl.BlockSpec((n_slc, n_rows), lambda bb, i: (0, 0)),
        ],
        out_specs=[
            pl.BlockSpec((None, tq, NSA_HEADS * NSA_HEAD_DIM), lambda bb, i: (bb, i, 0)),
            pl.BlockSpec((None, g, n_slc, tq), lambda bb, i: (bb, 0, 0, i)),
            pl.BlockSpec((None, None, 8, n_slc), lambda bb, i: (bb, i, 0, 0)),
        ],
        scratch_shapes=[pltpu.VMEM((n_rows, NSA_HEADS * tq), F32)],
        compiler_params=_cparams(("arbitrary", "arbitrary")),
        name="nsa_cmp_topk",
    )(slopes, qkv, kvc, vct, gates, ovt)


def _nsa_sel_win_kernel(bits_ref, slope_ref, q_ref, kvs_ref, vst_ref, kvw_ref, vwt_ref, selt_ref, oc_ref,
                        gates_ref, eselt_ref, o_ref, list_ref, sd_ref, *scratch,
                        tq, tk, win_span, n_chunks, n_streams):
    b = pl.program_id(0)
    g = pl.program_id(1)
    i = pl.program_id(2)
    hpg = NSA_HPG
    d = NSA_HEAD_DIM
    wide = hpg * tq
    wb_refs, s_refs, mc_refs, st_refs, acc_refs = (scratch[n::5] for n in range(5))
    lane = lax.broadcasted_iota(jnp.int32, (tq, LANES), 1)
    slopes = [slope_ref[g * hpg + hh] for hh in range(hpg)]
    q0s = [(i * n_streams + n) * tq for n in range(n_streams)]

    def per_head_row(vals):
        return jnp.concatenate([jnp.full((1, tq), v, F32) for v in vals], axis=1)

    q4s = [jnp.concatenate(
        [_head_query(q_ref[n * tq:(n + 1) * tq, (hh // 2) * LANES:(hh // 2 + 1) * LANES], hh % 2, lane)
         for hh in range(hpg)], axis=0) for n in range(n_streams)]

    rel = (lax.broadcasted_iota(jnp.int32, (tk, tq), 1)
           - lax.broadcasted_iota(jnp.int32, (tk, tq), 0))

    @pl.when(i == 0)
    def _():
        rel_f = rel.astype(F32)
        sd_ref[...] = jnp.concatenate([rel_f * slopes[hh] for hh in range(hpg)], axis=1)

    bits = bits_ref[(b * pl.num_programs(1) + g) * pl.num_programs(2) + i]

    def collect(j, n):
        hit = lax.shift_right_logical(bits, j) & 1

        @pl.when(hit == 1)
        def _():
            list_ref[n] = j
        return n + hit

    n_causal = (q0s[-1] + tq + tk - 1) // tk
    n_act = lax.fori_loop(0, n_causal, collect, 0)

    for n in range(n_streams):
        st_refs[n][0:1, :] = jnp.full((1, wide), M_FLOOR, F32)
        st_refs[n][1:2, :] = jnp.zeros((1, wide), F32)
        acc_refs[n][...] = jnp.zeros(acc_refs[n].shape, F32)
    selts = [selt_ref[:, n * tq:(n + 1) * tq] for n in range(n_streams)]

    def scores(n, j):
        kv = kvs_ref[pl.ds(pl.multiple_of(j * tk, tk), tk), :]
        return _dot_nt(kv, q4s[n])

    def produce(n, j, raw, slot):
        picked = _dot(eselt_ref[j], selts[n])
        causal = rel + (q0s[n] - j * tk) >= 0
        madd = jnp.where((picked > 0.5) & causal, 0.0, NEG_INF)
        s = raw - sd_ref[...] + jnp.concatenate([madd] * hpg, axis=1)
        s_refs[n][slot] = s
        mc_refs[n][slot, 0:1, :] = jnp.max(s, axis=0, keepdims=True)

    def consume(n, j, slot):
        off = per_head_row([slopes[hh] * (q0s[n] - j * tk).astype(F32) for hh in range(hpg)])
        st_ref, acc_ref = st_refs[n], acc_refs[n]
        m_prev = st_ref[0:1, :]
        l_prev = st_ref[1:2, :]
        m_next = jnp.maximum(m_prev, mc_refs[n][slot, 0:1, :] - off)
        alpha = jnp.exp(m_prev - m_next)
        p = jnp.exp(s_refs[n][slot] - (m_next + off))
        st_ref[0:1, :] = m_next
        st_ref[1:2, :] = alpha * l_prev + jnp.sum(p, axis=0, keepdims=True)
        vt = vst_ref[:, pl.ds(pl.multiple_of(j * tk, tk), tk)]
        acc_ref[...] = alpha * acc_ref[...] + _dot(vt, p.astype(BF16))

    j_first = list_ref[0]
    for n in range(n_streams):
        produce(n, j_first, scores(n, j_first), 0)

    def body(t, carry):
        j_new = list_ref[t]
        j_old = list_ref[t - 1]
        raws = [scores(n, j_new) for n in range(n_streams)]
        for n in range(n_streams):
            consume(n, j_old, (t - 1) & 1)
            produce(n, j_new, raws[n], t & 1)
        return carry

    lax.fori_loop(1, n_act, body, 0)
    j_last = list_ref[n_act - 1]

    for n in range(n_streams):
        q0 = q0s[n]
        consume(n, j_last, (n_act - 1) & 1)

        start = pl.multiple_of(jnp.maximum(q0 + tq - win_span, 0), tq)
        s_w = _dot_nt(kvw_ref[pl.ds(start, win_span), :], q4s[n])

        @pl.when(q0 + tq - win_span < n_streams * tq)
        def _():
            dist_w = (lax.broadcasted_iota(jnp.int32, (win_span, tq), 1)
                      - lax.broadcasted_iota(jnp.int32, (win_span, tq), 0) + (q0 - start))
            valid_w = (dist_w >= 0) & (dist_w < NSA_WINDOW)
            dist_wf = dist_w.astype(F32)
            wb_refs[n][...] = jnp.concatenate(
                [jnp.where(valid_w, -slopes[hh] * dist_wf, NEG_INF) for hh in range(hpg)], axis=1)

        s_w = s_w + wb_refs[n][...]
        p_w = jnp.exp(s_w - jnp.max(s_w, axis=0, keepdims=True))
        l_w = jnp.sum(p_w, axis=0, keepdims=True)
        o_w = _dot(vwt_ref[:, pl.ds(start, win_span)], p_w.astype(BF16)) * (1.0 / l_w)

        l_s = st_refs[n][1:2, :]
        o_s = acc_refs[n][...] * jnp.where(l_s > 0.0, 1.0 / l_s, 0.0)
        gates_t = gates_ref[n * tq:(n + 1) * tq, :].T
        g_s = jnp.concatenate([gates_t[3 * hh + 1:3 * hh + 2] for hh in range(hpg)], axis=1)
        g_w = jnp.concatenate([gates_t[3 * hh + 2:3 * hh + 3] for hh in range(hpg)], axis=1)
        comb = g_s * o_s + g_w * o_w
        by_head = jnp.concatenate([comb[:, hh * tq:(hh + 1) * tq] for hh in range(hpg)], axis=0)
        o_ref[n * tq:(n + 1) * tq, :] = (by_head.T + oc_ref[n * tq:(n + 1) * tq, :]).astype(o_ref.dtype)


def _nsa_sel_win(bits, qkv, vst, vwt, selt, oc, gates, eselt, tq, tk):
    b, t, _ = qkv.shape
    g = NSA_GROUPS
    d = NSA_HEAD_DIM
    n_slc = t // NSA_SLC_LEN
    nq = t // tq
    hd = NSA_HPG * d
    wide = NSA_HPG * tq
    kv_col0 = NSA_HEADS * d // LANES
    win_span = min(NSA_WINDOW + tq, t)
    n_chunks = t // tk
    ns = SEL_STREAMS if nq % SEL_STREAMS == 0 else 1
    rows = ns * tq
    tile_bits = bits.reshape(-1, ns)
    bits = functools.reduce(jnp.bitwise_or, [tile_bits[:, n] for n in range(ns)])
    slopes = jnp.asarray(_alibi_slopes(NSA_HEADS))
    per_stream = [
        pltpu.VMEM((win_span, wide), F32),
        pltpu.VMEM((2, tk, wide), F32),
        pltpu.VMEM((2, 8, wide), F32),
        pltpu.VMEM((8, wide), F32),
        pltpu.VMEM((d, wide), F32),
    ]
    grid_spec = pltpu.PrefetchScalarGridSpec(
        num_scalar_prefetch=1,
        grid=(b, g, nq // ns),
        in_specs=[
            pl.BlockSpec(memory_space=pltpu.SMEM),
            pl.BlockSpec((None, rows, hd), lambda bb, gg, i, bits: (bb, i, gg)),
            pl.BlockSpec((None, t, LANES), lambda bb, gg, i, bits: (bb, 0, kv_col0 + gg)),
            pl.BlockSpec((None, d, t), lambda bb, gg, i, bits: (bb, gg, 0)),
            pl.BlockSpec((None, t, LANES), lambda bb, gg, i, bits: (bb, 0, kv_col0 + g + gg)),
            pl.BlockSpec((None, d, t), lambda bb, gg, i, bits: (bb, gg, 0)),
            pl.BlockSpec((None, None, n_slc, rows), lambda bb, gg, i, bits: (bb, gg, 0, i)),
            pl.BlockSpec((None, rows, hd), lambda bb, gg, i, bits: (bb, i, gg)),
            pl.BlockSpec((None, rows, LANES), lambda bb, gg, i, bits: (bb, i, gg)),
            pl.BlockSpec(eselt.shape, lambda bb, gg, i, bits: (0, 0, 0)),
        ],
        out_specs=pl.BlockSpec((None, rows, hd), lambda bb, gg, i, bits: (bb, i, gg)),
        scratch_shapes=[pltpu.SMEM((n_chunks,), jnp.int32), pltpu.VMEM((tk, wide), F32)] + per_stream * ns,
    )
    return pl.pallas_call(
        functools.partial(_nsa_sel_win_kernel, tq=tq, tk=tk, win_span=win_span, n_chunks=n_chunks,
                          n_streams=ns),
        out_shape=jax.ShapeDtypeStruct((b, t, NSA_HEADS * d), BF16),
        grid_spec=grid_spec,
        compiler_params=_cparams(("arbitrary", "arbitrary", "arbitrary")),
        name="nsa_sel_win",
    )(bits, slopes, qkv, qkv, vst, qkv, vwt, selt, oc, gates, eselt)


def _nsa_weight_layout():
    hd, g, d = NSA_HEADS * NSA_HEAD_DIM, NSA_GROUPS, NSA_HEAD_DIM
    base = {name: hd + n * g * d for n, name in enumerate(["kc", "vc", "ks", "vs", "kw", "vw"])}
    gate0 = hd + 6 * g * d
    cols = list(range(hd))
    for kn, vn in (("ks", "vs"), ("kw", "vw"), ("kc", "vc")):
        for gg in range(g):
            cols += list(range(base[kn] + gg * d, base[kn] + (gg + 1) * d))
            cols += list(range(base[vn] + gg * d, base[vn] + (gg + 1) * d))
    for gg in range(g):
        n_gate = NSA_HPG * 3
        cols += list(range(gate0 + gg * n_gate, gate0 + (gg + 1) * n_gate)) + [-1] * (LANES - n_gate)
    return np.asarray(cols, np.int32)


def _permute_columns(w, cols):
    picked = jnp.take(w, jnp.asarray(np.maximum(cols, 0)), axis=1)
    return jnp.where(jnp.asarray(cols >= 0)[None, :], picked, 0.0)


def _overlap_t(n_rows, n_slc):
    n_cmp = n_rows - 1
    c_start = np.arange(n_rows) * NSA_CMP_STRIDE
    s_start = np.arange(n_slc) * NSA_SLC_LEN
    ov = (np.minimum(c_start[:, None] + NSA_CMP_LEN, s_start[None, :] + NSA_SLC_LEN)
          - np.maximum(c_start[:, None], s_start[None, :]))
    ov = np.clip(ov, 0, None).astype(np.float32) / NSA_CMP_LEN
    ov[n_cmp:] = 0.0
    return jnp.asarray(ov.T, BF16)


def _block_expander(n_slc, tk):
    n_chunks = n_slc * NSA_SLC_LEN // tk
    key_blk = (np.arange(n_chunks)[:, None] * tk + np.arange(tk)[None, :]) // NSA_SLC_LEN
    e = (key_blk[:, :, None] == np.arange(n_slc)[None, None, :]).astype(np.float32)
    return jnp.asarray(e, BF16)


def _chunk_bits(any_sel, sub_per_tile, blocks_per_chunk):
    b, nq, _, n_slc = any_sel.shape
    g = NSA_GROUPS
    n_chunks = n_slc // blocks_per_chunk
    a = any_sel.reshape(b, nq, sub_per_tile, g, n_chunks, blocks_per_chunk).max(axis=-1) > 0.5
    a = a.transpose(0, 3, 1, 2, 4).reshape(b * g * nq * sub_per_tile, n_chunks)
    weights = jnp.left_shift(jnp.uint32(1), jnp.arange(n_chunks, dtype=jnp.uint32))
    return lax.bitcast_convert_type(jnp.sum(jnp.where(a, weights[None, :], jnp.uint32(0)), axis=1,
                                            dtype=jnp.uint32), jnp.int32)


def _diff_mixer(x, mod, layer, w_in, w_out, lam_q1, lam_k1, lam_q2, lam_k2, subln_g):
    n_qk = 2 * DIFF_HEADS * 2 * DIFF_HEAD_DIM
    n_v = DIFF_HEADS * DIFF_V_DIM
    w = w_in.astype(BF16)
    qk, vt = _modulated_projection(x, mod, layer, w[:, :n_qk], [(0, n_qk, "id")], [BF16],
                                   w[:, n_qk:].T, [(0, n_v)])
    lam_vecs = jnp.stack([lam_q1, lam_k1, lam_q2, lam_k2]).astype(F32)
    o = _diff_attention(qk, vt, lam_vecs, subln_g.astype(F32), layer)
    return o, w_out.astype(BF16)


def _nsa_mixer(x, mod, layer, w_in, pos_k, pos_v, k_w1, k_w2, v_w1, v_w2, w_out):
    b, t, _ = x.shape
    g, d = NSA_GROUPS, NSA_HEAD_DIM
    hd = NSA_HEADS * d
    cols = _nsa_weight_layout()
    w = _permute_columns(w_in, cols).astype(BF16)
    n_main = hd + 2 * (2 * g * d)
    n_cmp_cols = 2 * g * d
    plan = [(0, n_main, "id"), (n_main, n_main + n_cmp_cols, "id"),
            (n_main + n_cmp_cols, n_main + n_cmp_cols + g * LANES, "sigmoid")]
    v_cols = np.concatenate([np.arange(hd + 6 * g * d)[hd + (2 * n + 1) * g * d: hd + (2 * n + 2) * g * d]
                             for n in (1, 2)])
    wt = jnp.take(w_in, jnp.asarray(v_cols), axis=1).T.astype(BF16)
    qkv, kvc_tok, gates, vst, vwt = _modulated_projection(
        x, mod, layer, w, plan, [BF16, F32, F32], wt, [(0, g * d), (g * d, 2 * g * d)])

    rows = t // NSA_CMP_STRIDE
    pos = jnp.stack([pos_k.reshape(1, -1), pos_v.reshape(1, -1)]).astype(F32)
    w1 = jnp.stack([k_w1, v_w1]).astype(BF16)
    zeros = jnp.zeros((NSA_CMP_HIDDEN, d), F32)
    w2pad = jnp.stack([jnp.concatenate([k_w2, zeros], axis=1),
                       jnp.concatenate([zeros, v_w2], axis=1)]).astype(BF16)
    kvc, vct = _compress(kvc_tok, pos, w1, w2pad)

    n_slc = t // NSA_SLC_LEN
    tq_sel = min(SEL_Q_TILE, t)
    tq_cmp = min(CMP_Q_TILE, t)
    tk = min(SEL_K_TILE, t)
    oc, selt, any_sel = _nsa_cmp_topk(qkv, kvc, vct, gates, _overlap_t(rows, n_slc), tq_cmp, tq_sel)
    bits = _chunk_bits(any_sel, tq_cmp // tq_sel, tk // NSA_SLC_LEN)
    o = _nsa_sel_win(bits, qkv, vst, vwt, selt, oc, gates, _block_expander(n_slc, tk), tq_sel, tk)
    return o, w_out.astype(BF16)


def kernel(x, c, diff_w_in, diff_w_out, diff_lam_q1, diff_lam_k1, diff_lam_q2, diff_lam_k2, diff_subln_g,
           nsa_w_in, nsa_cmp_pos_k, nsa_cmp_pos_v, nsa_cmp_k_w1, nsa_cmp_k_w2, nsa_cmp_v_w1, nsa_cmp_v_w2,
           nsa_w_out, ada_w, ada_b, ln_mix_g, ln_mix_b, ln_ffn_g, ln_ffn_b, ffn_w_gu, ffn_w_down):
    mod = _ada_modulation(c, ada_w, ada_b)
    for i in range(DEPTH):
        j = i // N_MIXERS
        if i % N_MIXERS == 0:
            o, w_out = _diff_mixer(x, mod, i, diff_w_in[j], diff_w_out[j], diff_lam_q1[j], diff_lam_k1[j],
                                   diff_lam_q2[j], diff_lam_k2[j], diff_subln_g[j])
        else:
            o, w_out = _nsa_mixer(x, mod, i, nsa_w_in[j], nsa_cmp_pos_k[j], nsa_cmp_pos_v[j],
                                  nsa_cmp_k_w1[j], nsa_cmp_k_w2[j], nsa_cmp_v_w1[j], nsa_cmp_v_w2[j],
                                  nsa_w_out[j])
        x = _mix_ffn(o, w_out, x, mod, i, ln_mix_g[i], ln_mix_b[i],
                     ffn_w_gu[i].astype(BF16), ffn_w_down[i].astype(BF16), ln_ffn_g[i], ln_ffn_b[i])
    return x
```

```python
import functools
import math

import numpy as np
import jax
import jax.numpy as jnp
from jax import lax
from jax.experimental import pallas as pl
from jax.experimental.pallas import tpu as pltpu

F32 = jnp.float32
BF16 = jnp.bfloat16

D_MODEL = 1024
DEPTH = 2
N_MIXERS = 2
DEEPNORM_ALPHA = (2 * DEPTH) ** 0.25
LN_EPS = 1e-5
RMS_EPS = 1e-5
NEG_INF = -1e30
FORCED_SCORE = 1e6

DIFF_HEADS = 8
DIFF_HEAD_DIM = 64
DIFF_V_DIM = 2 * DIFF_HEAD_DIM

NSA_HEADS = 16
NSA_GROUPS = 4
NSA_HEAD_DIM = 64
NSA_HPG = NSA_HEADS // NSA_GROUPS
NSA_CMP_LEN = 32
NSA_CMP_STRIDE = 16
NSA_CMP_HIDDEN = 128
NSA_SLC_LEN = 64
NSA_SLC_TOPK = 16
NSA_WINDOW = 512

FFN_HIDDEN = -(-8 * D_MODEL // (3 * 256)) * 256

LANES = 128
VMEM_LIMIT = 56 * 1024 * 1024
PROJ_ROWS = 512
FFN_CHUNK = FFN_HIDDEN // 2
DIFF_TILE = 256
DIFF_STREAMS = 2
DIFF_ELIDE_LOG = 110.0
DIFF_ELIDE_SLACK = 1.02
CMP_Q_TILE = 256
SEL_Q_TILE = 128
SEL_K_TILE = 256
SEL_STREAMS = 2
M_FLOOR = -1e29
TAKEN = -3e38
ONES_ROWS = 16


def _cparams(sem):
    return pltpu.CompilerParams(dimension_semantics=sem, vmem_limit_bytes=VMEM_LIMIT)


def _dot(a, b):
    return jnp.dot(a, b, preferred_element_type=F32)


def _dot_nt(a, b):
    return lax.dot_general(a, b, (((1,), (1,)), ((), ())), preferred_element_type=F32)


def _layer_norm(z, g, b):
    mu = jnp.mean(z, axis=-1, keepdims=True)
    zc = z - mu
    var = jnp.mean(zc * zc, axis=-1, keepdims=True)
    return zc * lax.rsqrt(var + LN_EPS) * g + b


def _alibi_slopes(n_heads):
    return (2.0 ** (-8.0 * np.arange(1, n_heads + 1, dtype=np.float32) / n_heads)).astype(np.float32)


def _ada_kernel(c_ref, w_ref, b_ref, o_ref):
    c = c_ref[...]
    s = (c * jax.nn.sigmoid(c)).astype(BF16)
    o_ref[0] = _dot(s, w_ref[0].astype(BF16)) + b_ref[0]


def _ada_modulation(c, ada_w, ada_b):
    depth, d, d6 = ada_w.shape
    b = c.shape[0]
    rows = 8
    c_pad = jnp.zeros((rows, d), F32).at[:b].set(c)
    out = pl.pallas_call(
        _ada_kernel,
        out_shape=jax.ShapeDtypeStruct((depth, rows, d6), F32),
        grid=(depth, d6 // d),
        in_specs=[
            pl.BlockSpec((rows, d), lambda l, j: (0, 0)),
            pl.BlockSpec((1, d, d), lambda l, j: (l, 0, j)),
            pl.BlockSpec((1, 1, d), lambda l, j: (l, 0, j)),
        ],
        out_specs=pl.BlockSpec((1, rows, d), lambda l, j: (l, 0, j)),
        compiler_params=_cparams(("arbitrary", "arbitrary")),
        name="ada_modulation",
    )(c_pad, ada_w, ada_b.reshape(depth, 1, d6))
    return out[:, :b].reshape(depth, b, 6, 1, d)


def _mod_spec(layer, which, n_grid):
    d = D_MODEL
    if n_grid == 2:
        return pl.BlockSpec((None, None, None, 1, d), lambda b, i: (layer, b, which, 0, 0))
    return pl.BlockSpec((None, None, None, 1, d), lambda b, i, k: (layer, b, which, 0, 0))


def _proj_kernel(x_ref, sc_ref, sh_ref, w_ref, wt_ref, *out_refs, plan, t_plan):
    h = (x_ref[...] * (1.0 + sc_ref[...]) + sh_ref[...]).astype(BF16)
    for ref, (c0, c1, kind), in zip(out_refs, plan):
        step = 1024
        for s0 in range(c0, c1, step):
            s1 = min(s0 + step, c1)
            y = _dot(h, w_ref[:, s0:s1])
            if kind == "sigmoid":
                y = jax.nn.sigmoid(y)
            ref[:, s0 - c0:s1 - c0] = y.astype(ref.dtype)
    for ref, (r0, r1) in zip(out_refs[len(plan):], t_plan):
        ref[...] = _dot_nt(wt_ref[r0:r1, :], h).astype(ref.dtype)


def _modulated_projection(x, mod, layer, w, plan, out_dtypes, wt, t_plan):
    b, t, d = x.shape
    n = w.shape[1]
    tm = min(PROJ_ROWS, t)
    out_shape = [jax.ShapeDtypeStruct((b, t, c1 - c0), dt) for (c0, c1, _), dt in zip(plan, out_dtypes)]
    out_specs = [pl.BlockSpec((None, tm, c1 - c0), lambda bb, i: (bb, i, 0)) for (c0, c1, _) in plan]
    out_shape += [jax.ShapeDtypeStruct((b, r1 - r0, t), BF16) for (r0, r1) in t_plan]
    out_specs += [pl.BlockSpec((None, r1 - r0, tm), lambda bb, i: (bb, 0, i)) for (r0, r1) in t_plan]
    return pl.pallas_call(
        functools.partial(_proj_kernel, plan=plan, t_plan=t_plan),
        out_shape=out_shape,
        grid=(b, t // tm),
        in_specs=[
            pl.BlockSpec((None, tm, d), lambda bb, i: (bb, i, 0)),
            _mod_spec(layer, 1, 2),
            _mod_spec(layer, 0, 2),
            pl.BlockSpec((d, n), lambda bb, i: (0, 0)),
            pl.BlockSpec(wt.shape, lambda bb, i: (0, 0)),
        ],
        out_specs=out_specs,
        compiler_params=_cparams(("arbitrary", "arbitrary")),
        name=f"mod_proj_l{layer}",
    )(x, mod, mod, w, wt)


def _diff_attn_kernel(slope_ref, lam_ref, g_ref, q_ref, k_ref, vt_ref, o_ref, kn_ref, *scratch,
                      tile, lam_init, n_streams):
    h = pl.program_id(1)
    qi = pl.program_id(2)
    slope = slope_ref[0, h]
    inv_slope = slope_ref[1, h]
    half = DIFF_HEAD_DIM
    s_refs, mc_refs, st_refs, acc_refs = (scratch[n::4] for n in range(4))

    key = lax.broadcasted_iota(jnp.int32, (tile, tile), 0)
    qry = lax.broadcasted_iota(jnp.int32, (tile, tile), 1)
    lane = lax.broadcasted_iota(jnp.int32, (tile, LANES), 1)
    key_pos = lax.broadcasted_iota(jnp.int32, (tile, LANES), 0).astype(F32) * slope
    k_extra = jnp.where(lane == 0, 1.0, jnp.where(lane == 1, key_pos, 0.0)).astype(BF16)
    row = lax.broadcasted_iota(jnp.int32, (LANES, tile), 0)
    qry_pos = lax.broadcasted_iota(jnp.int32, (LANES, tile), 1).astype(F32) * slope
    q_extra = jnp.where(row == 0, -qry_pos, jnp.where(row == 1, 1.0, 0.0)).astype(BF16)
    ones_rows = jnp.ones((ONES_ROWS, tile), BF16)

    def max_component_norm(x, lane_ids):
        sq = x * x
        lo = jnp.sum(jnp.where(lane_ids < half, sq, 0.0), axis=1, keepdims=True)
        hi = jnp.sum(jnp.where(lane_ids >= half, sq, 0.0), axis=1, keepdims=True)
        return jnp.max(jnp.sqrt(jnp.max(jnp.maximum(lo, hi), axis=0, keepdims=True)))

    @pl.when(qi == 0)
    def _():
        for n in range(n_streams):
            kf = k_ref[n].astype(F32)
            kn_ref[n] = max_component_norm(kf, lax.broadcasted_iota(jnp.int32, kf.shape, 1))

    qcs = []
    reach_max = jnp.float32(0.0)
    for n in range(n_streams):
        qf = q_ref[n].astype(F32) * (DIFF_HEAD_DIM ** -0.5)
        qcs.append(tuple(jnp.concatenate([jnp.where(sel, qf, 0.0).T.astype(BF16), q_extra], axis=0)
                         for sel in (lane < half, lane >= half)))
        qn = max_component_norm(qf, lane)
        reach = (DIFF_ELIDE_LOG + DIFF_ELIDE_SLACK * 2.0 * qn * kn_ref[n]) * inv_slope
        reach_max = jnp.maximum(reach_max, reach)
        acc_refs[n][...] = jnp.zeros(acc_refs[n].shape, F32)
        st_refs[n][...] = jnp.full(st_refs[n].shape, M_FLOOR, F32)
    n_keep = lax.fori_loop(
        0, qi, lambda a, cnt: cnt + jnp.where((a * tile + 1).astype(F32) < reach_max, 1, 0), 0)
    j0 = qi - n_keep

    def scores(n, j):
        k = k_ref[n, pl.ds(pl.multiple_of(j * tile, tile), tile), :]
        k = jnp.concatenate([k, k_extra], axis=1)
        return [_dot(k, qcs[n][c]) for c in range(2)]

    def produce(n, j, raw, diagonal):
        slot = j & 1
        for c in range(2):
            s = raw[c]
            if diagonal is True:
                s = jnp.where(key <= qry, s, NEG_INF)
            elif diagonal is not False:
                s = jnp.where((key <= qry) | jnp.logical_not(diagonal), s, NEG_INF)
            s_refs[n][slot, c] = s
            mc_refs[n][slot, c:c + 1, :] = jnp.max(s, axis=0, keepdims=True)

    def consume(n, j):
        slot = j & 1
        vt = vt_ref[n, :, pl.ds(pl.multiple_of(j * tile, tile), tile)]
        vt = jnp.concatenate([vt, ones_rows], axis=0)
        off = slope * ((qi - j) * tile).astype(F32)
        st_ref, acc_ref = st_refs[n], acc_refs[n]
        for c in range(2):
            m_prev = st_ref[c:c + 1, :]
            m_next = jnp.maximum(m_prev, mc_refs[n][slot, c:c + 1, :] - off)
            alpha = jnp.exp(m_prev - m_next)
            p = jnp.exp(s_refs[n][slot, c] - (m_next + off))
            st_ref[c:c + 1, :] = m_next
            acc_ref[c] = alpha * acc_ref[c] + _dot(vt, p.astype(BF16))

    for n in range(n_streams):
        produce(n, j0, scores(n, j0), j0 == qi)

    def step(t, diagonal):
        raws = [scores(n, t) for n in range(n_streams)]
        for n in range(n_streams):
            consume(n, t - 1)
            produce(n, t, raws[n], diagonal)

    def body(t, carry):
        step(t, False)
        return carry

    lax.fori_loop(j0 + 1, qi, body, 0)

    @pl.when(qi > j0)
    def _():
        step(qi, True)

    lam_v = lam_ref[...]
    lam = (jnp.exp(jnp.sum(lam_v[0:1] * lam_v[1:2], axis=1, keepdims=True))
           - jnp.exp(jnp.sum(lam_v[2:3] * lam_v[3:4], axis=1, keepdims=True)) + lam_init)
    for n in range(n_streams):
        consume(n, qi)
        dv = DIFF_V_DIM
        l0 = acc_refs[n][0, dv:dv + 1, :]
        l1 = acc_refs[n][1, dv:dv + 1, :]
        o = acc_refs[n][0, :dv, :] / l0 - lam * (acc_refs[n][1, :dv, :] / l1)
        o = o * lax.rsqrt(jnp.mean(o * o, axis=0, keepdims=True) + RMS_EPS)
        o_ref[n] = ((o * g_ref[...] * (1.0 - lam_init)).T).astype(o_ref.dtype)


def _diff_attention(qk, vt, lam_vecs, subln_g, layer_idx):
    b, t, _ = qk.shape
    hh = DIFF_HEADS
    tile = min(DIFF_TILE, t)
    ns = DIFF_STREAMS if b % DIFF_STREAMS == 0 else 1
    lam_init = 0.8 - 0.6 * math.exp(-0.3 * layer_idx)
    slopes = _alibi_slopes(hh)
    slope_tab = jnp.asarray(np.stack([slopes, 1.0 / slopes]))
    per_stream = [
        pltpu.VMEM((2, 2, tile, tile), F32),
        pltpu.VMEM((2, 8, tile), F32),
        pltpu.VMEM((8, tile), F32),
        pltpu.VMEM((2, DIFF_V_DIM + ONES_ROWS, tile), F32),
    ]
    col = (slopes[:, None] * np.arange(tile, dtype=np.float32)[None, :]).astype(np.float32)
    assert np.array_equal(col.astype(jnp.bfloat16).astype(np.float32), col)
    return pl.pallas_call(
        functools.partial(_diff_attn_kernel, tile=tile, lam_init=lam_init, n_streams=ns),
        out_shape=jax.ShapeDtypeStruct((b, t, hh * DIFF_V_DIM), BF16),
        grid=(b // ns, hh, t // tile),
        in_specs=[
            pl.BlockSpec(memory_space=pltpu.SMEM),
            pl.BlockSpec((4, DIFF_HEAD_DIM), lambda bb, h, i: (0, 0)),
            pl.BlockSpec((DIFF_V_DIM, 1), lambda bb, h, i: (0, 0)),
            pl.BlockSpec((ns, tile, LANES), lambda bb, h, i: (bb, i, h)),
            pl.BlockSpec((ns, t, LANES), lambda bb, h, i: (bb, 0, hh + h)),
            pl.BlockSpec((ns, DIFF_V_DIM, t), lambda bb, h, i: (bb, h, 0)),
        ],
        out_specs=pl.BlockSpec((ns, tile, LANES), lambda bb, h, i: (bb, i, h)),
        scratch_shapes=[pltpu.SMEM((ns,), F32)] + per_stream * ns,
        compiler_params=_cparams(("arbitrary", "arbitrary", "arbitrary")),
        name="diff_attention",
    )(slope_tab, lam_vecs, subln_g.reshape(DIFF_V_DIM, 1), qk, qk, vt)


def _mix_ffn_kernel(o_ref, wo_ref, x_ref, ga_ref, lng_a_ref, lnb_a_ref, sc_ref, sh_ref, gf_ref,
                    wg_ref, wu_ref, wd_ref, lng_f_ref, lnb_f_ref, out_ref, x1_ref, h_ref, acc_ref):
    k = pl.program_id(2)

    @pl.when(k == 0)
    def _():
        y = _dot(o_ref[...], wo_ref[...])
        z = DEEPNORM_ALPHA * x_ref[...] + (1.0 + ga_ref[...]) * y
        x1 = _layer_norm(z, lng_a_ref[...], lnb_a_ref[...])
        x1_ref[...] = x1
        h_ref[...] = (x1 * (1.0 + sc_ref[...]) + sh_ref[...]).astype(BF16)
        acc_ref[...] = jnp.zeros(acc_ref.shape, F32)

    h = h_ref[...]
    gp = _dot(h, wg_ref[...])
    up = _dot(h, wu_ref[...])
    act = (gp * jax.nn.sigmoid(gp) * up).astype(BF16)
    acc_ref[...] += _dot(act, wd_ref[...])

    @pl.when(k == pl.num_programs(2) - 1)
    def _():
        z = DEEPNORM_ALPHA * x1_ref[...] + (1.0 + gf_ref[...]) * acc_ref[...]
        out_ref[...] = _layer_norm(z, lng_f_ref[...], lnb_f_ref[...])


def _mix_ffn(o, w_out, x, mod, layer, ln_mix_g, ln_mix_b, w_gu, w_down, ln_ffn_g, ln_ffn_b):
    b, t, d = x.shape
    dh = o.shape[-1]
    fh = w_down.shape[0]
    th = min(FFN_CHUNK, fh)
    nh = fh // th
    tm = min(PROJ_ROWS, t)
    row_vec = pl.BlockSpec((1, d), lambda bb, i, k: (0, 0))
    return pl.pallas_call(
        _mix_ffn_kernel,
        out_shape=jax.ShapeDtypeStruct((b, t, d), F32),
        grid=(b, t // tm, nh),
        in_specs=[
            pl.BlockSpec((None, tm, dh), lambda bb, i, k: (bb, i, 0)),
            pl.BlockSpec((dh, d), lambda bb, i, k: (0, 0)),
            pl.BlockSpec((None, tm, d), lambda bb, i, k: (bb, i, 0)),
            _mod_spec(layer, 2, 3),
            row_vec,
            row_vec,
            _mod_spec(layer, 4, 3),
            _mod_spec(layer, 3, 3),
            _mod_spec(layer, 5, 3),
            pl.BlockSpec((d, th), lambda bb, i, k: (0, k)),
            pl.BlockSpec((d, th), lambda bb, i, k: (0, nh + k)),
            pl.BlockSpec((th, d), lambda bb, i, k: (k, 0)),
            row_vec,
            row_vec,
        ],
        out_specs=pl.BlockSpec((None, tm, d), lambda bb, i, k: (bb, i, 0)),
        scratch_shapes=[pltpu.VMEM((tm, d), F32), pltpu.VMEM((tm, d), BF16), pltpu.VMEM((tm, d), F32)],
        compiler_params=_cparams(("arbitrary", "arbitrary", "arbitrary")),
        name=f"mix_ffn_l{layer}",
    )(o, w_out, x, mod, ln_mix_g.reshape(1, d), ln_mix_b.reshape(1, d), mod, mod, mod,
      w_gu, w_gu, w_down, ln_ffn_g.reshape(1, d), ln_ffn_b.reshape(1, d))


def _gelu_tanh(x):
    return 0.5 * x * (1.0 + jnp.tanh(math.sqrt(2.0 / math.pi) * (x + 0.044715 * (x * x * x))))


def _compress_kernel(x_ref, pos_ref, w1_ref, w2_ref, o_ref, vt_ref):
    d = NSA_HEAD_DIM
    n_rows, stride, _ = x_ref.shape
    half = stride * d
    lane = lax.broadcasted_iota(jnp.int32, (n_rows, LANES), 1)
    pieces = ([], [])
    for lp in range(stride // 2):
        even = x_ref[:, 2 * lp, :]
        odd = x_ref[:, 2 * lp + 1, :]
        pieces[0].append(jnp.where(lane < d, even, pltpu.roll(odd, d, 1)))
        pieces[1].append(jnp.where(lane < d, pltpu.roll(even, d, 1), odd))
    out = None
    for kv in range(2):
        r = jnp.concatenate(pieces[kv], axis=1)
        r_next = pltpu.roll(r, n_rows - 1, 0)
        a = (r + pos_ref[kv, :, :half]).astype(BF16)
        bnx = (r_next + pos_ref[kv, :, half:]).astype(BF16)
        hid = _dot(a, w1_ref[kv, :half, :]) + _dot(bnx, w1_ref[kv, half:, :])
        y = _dot(_gelu_tanh(hid).astype(BF16), w2_ref[kv])
        out = y if out is None else out + y
    o_ref[...] = out.astype(o_ref.dtype)
    vt_ref[...] = out.T[NSA_HEAD_DIM:, :].astype(vt_ref.dtype)


def _compress(kvc_tok, pos, w1, w2pad):
    b, t, _ = kvc_tok.shape
    g = NSA_GROUPS
    rows = t // NSA_CMP_STRIDE
    width = NSA_CMP_STRIDE * NSA_HEAD_DIM
    return pl.pallas_call(
        _compress_kernel,
        out_shape=[jax.ShapeDtypeStruct((b, g, rows, LANES), BF16),
                   jax.ShapeDtypeStruct((b, g, NSA_HEAD_DIM, rows), BF16)],
        grid=(b, g),
        in_specs=[
            pl.BlockSpec((None, rows, NSA_CMP_STRIDE, LANES), lambda bb, gg: (bb, 0, 0, gg)),
            pl.BlockSpec((2, 1, 2 * width), lambda bb, gg: (0, 0, 0)),
            pl.BlockSpec((2, 2 * width, NSA_CMP_HIDDEN), lambda bb, gg: (0, 0, 0)),
            pl.BlockSpec((2, NSA_CMP_HIDDEN, LANES), lambda bb, gg: (0, 0, 0)),
        ],
        out_specs=[pl.BlockSpec((None, None, rows, LANES), lambda bb, gg: (bb, gg, 0, 0)),
                   pl.BlockSpec((None, None, NSA_HEAD_DIM, rows), lambda bb, gg: (bb, gg, 0, 0))],
        compiler_params=_cparams(("arbitrary", "arbitrary")),
        name="nsa_compress",
    )(kvc_tok.reshape(b, rows, NSA_CMP_STRIDE, g * LANES), pos, w1, w2pad)


def _head_query_t(q_pair, head_in_pair, lane):
    d = NSA_HEAD_DIM
    qf = q_pair.astype(F32) * (d ** -0.5)
    if head_in_pair == 1:
        qf = pltpu.roll(qf, d, 1)
    return jnp.where(lane < d, qf, 0.0).T.astype(BF16)


def _merge_head_pair(o_even, o_odd, lane):
    return jnp.where(lane < NSA_HEAD_DIM, pltpu.roll(o_even, NSA_HEAD_DIM, 1), o_odd)


def _nsa_cmp_topk_kernel(slope_ref, q_ref, kvc_ref, vct_ref, gates_ref, ovt_ref,
                         oc_ref, sel_ref, any_ref, sdc_ref, *, tq, n_rows, n_slc, k_sel, sub_tile):
    i = pl.program_id(1)
    q0 = i * tq
    hpg = NSA_HPG
    d = NSA_HEAD_DIM
    lane = lax.broadcasted_iota(jnp.int32, (tq, LANES), 1)
    qry_c = lax.broadcasted_iota(jnp.int32, (n_rows, tq), 1)
    end_c = lax.broadcasted_iota(jnp.int32, (n_rows, tq), 0) * NSA_CMP_STRIDE + (NSA_CMP_LEN - 1)

    @pl.when(i == 0)
    def _():
        rel_c = (qry_c - end_c).astype(F32)
        for h in range(NSA_HEADS):
            sdc_ref[:, h * tq:(h + 1) * tq] = rel_c * slope_ref[h]

    valid = end_c <= q0 + qry_c

    blk_i = lax.broadcasted_iota(jnp.int32, (n_slc, tq), 0)
    qpos = q0 + lax.broadcasted_iota(jnp.int32, (n_slc, tq), 1)
    q_blk = lax.shift_right_logical(qpos, int(math.log2(NSA_SLC_LEN)))
    forced = (blk_i == 0) | (blk_i == q_blk) | (blk_i == q_blk - 1)
    causal_blk = blk_i * NSA_SLC_LEN <= qpos
    blk_f = blk_i.astype(F32)

    for g in range(NSA_GROUPS):
        q4t = jnp.concatenate(
            [_head_query_t(q_ref[:, ((g * hpg + hh) // 2) * LANES:((g * hpg + hh) // 2 + 1) * LANES], hh % 2, lane)
             for hh in range(hpg)], axis=1)
        raw = _dot(kvc_ref[g], q4t)
        s = jnp.concatenate(
            [jnp.where(valid, raw[:, hh * tq:(hh + 1) * tq]
                       - sdc_ref[:, (g * hpg + hh) * tq:(g * hpg + hh + 1) * tq], NEG_INF)
             for hh in range(hpg)], axis=1)
        m = jnp.max(s, axis=0, keepdims=True)
        p = jnp.exp(s - m)
        l = jnp.sum(p, axis=0, keepdims=True)
        inv = jnp.where(m > 0.5 * NEG_INF, 1.0 / l, 0.0)
        lhs = jnp.concatenate([vct_ref[g], ovt_ref[...]], axis=0)
        r = _dot(lhs, p.astype(BF16))
        gates_t = gates_ref[:, g * LANES:(g + 1) * LANES].T
        w = inv * jnp.concatenate([gates_t[3 * hh:3 * hh + 1] for hh in range(hpg)], axis=1)
        oc_t = r[:d] * w
        by_head = jnp.concatenate([oc_t[:, hh * tq:(hh + 1) * tq] for hh in range(hpg)], axis=0)
        oc_ref[:, g * hpg * d:(g + 1) * hpg * d] = by_head.T
        imp_t = r[d:, 0:tq] * inv[:, 0:tq]
        for hh in range(1, hpg):
            imp_t = imp_t + r[d:, hh * tq:(hh + 1) * tq] * inv[:, hh * tq:(hh + 1) * tq]

        imp = jnp.where(causal_blk, jnp.where(forced, FORCED_SCORE, imp_t), NEG_INF)
        for _ in range(k_sel):
            mx = jnp.max(imp, axis=0, keepdims=True)
            cand = jnp.where(imp == mx, blk_f, float(n_slc))
            pick = blk_f == jnp.min(cand, axis=0, keepdims=True)
            imp = jnp.where(pick, TAKEN, imp)
        sel_b = jnp.where(causal_blk & (imp < 0.5 * TAKEN), 1.0, 0.0).astype(BF16)
        sel_ref[g] = sel_b
        ones = jnp.ones((8, sub_tile), BF16)
        for sub in range(tq // sub_tile):
            r = sub * NSA_GROUPS + g
            cnt = _dot_nt(ones, sel_b[:, sub * sub_tile:(sub + 1) * sub_tile])
            any_ref[r:r + 1, :] = cnt[0:1]


def _nsa_cmp_topk(qkv, kvc, vct, gates, ovt, tq, sub_tile):
    b, t, _ = qkv.shape
    g = NSA_GROUPS
    n_rows = kvc.shape[2]
    n_slc = t // NSA_SLC_LEN
    k_sel = min(NSA_SLC_TOPK, n_slc)
    nq = t // tq
    assert n_slc == LANES and (tq // sub_tile) * g == 8
    slopes = jnp.asarray(_alibi_slopes(NSA_HEADS))
    return pl.pallas_call(
        functools.partial(_nsa_cmp_topk_kernel, tq=tq, n_rows=n_rows, n_slc=n_slc, k_sel=k_sel,
                          sub_tile=sub_tile),
        out_shape=[
            jax.ShapeDtypeStruct((b, t, NSA_HEADS * NSA_HEAD_DIM), F32),
            jax.ShapeDtypeStruct((b, g, n_slc, t), BF16),
            jax.ShapeDtypeStruct((b, nq, 8, n_slc), F32),
        ],
        grid=(b, nq),
        in_specs=[
            pl.BlockSpec(memory_space=pltpu.SMEM),
            pl.BlockSpec((None, tq, NSA_HEADS * NSA_HEAD_DIM), lambda bb, i: (bb, i, 0)),
            pl.BlockSpec((None, g, n_rows, LANES), lambda bb, i: (bb, 0, 0, 0)),
            pl.BlockSpec((None, g, NSA_HEAD_DIM, n_rows), lambda bb, i: (bb, 0, 0, 0)),
            pl.BlockSpec((None, tq, g * LANES), lambda bb, i: (bb, i, 0)),
            pl.BlockSpec((n_slc, n_rows), lambda bb, i: (0, 0)),
        ],
        out_specs=[
            pl.BlockSpec((None, tq, NSA_HEADS * NSA_HEAD_DIM), lambda bb, i: (bb, i, 0)),
            pl.BlockSpec((None, g, n_slc, tq), lambda bb, i: (bb, 0, 0, i)),
            pl.BlockSpec((None, None, 8, n_slc), lambda bb, i: (bb, i, 0, 0)),
        ],
        scratch_shapes=[pltpu.VMEM((n_rows, NSA_HEADS * tq), F32)],
        compiler_params=_cparams(("arbitrary", "arbitrary")),
        name="nsa_cmp_topk",
    )(slopes, qkv, kvc, vct, gates, ovt)


def _nsa_sel_win_kernel(bits_ref, slope_ref, q_ref, kvs_ref, vst_ref, kvw_ref, vwt_ref, selt_ref, oc_ref,
                        gates_ref, eselt_ref, o_ref, list_ref, sd_ref, *scratch,
                        tq, tk, win_span, n_chunks, n_streams):
    b = pl.program_id(0)
    g = pl.program_id(1)
    i = pl.program_id(2)
    hpg = NSA_HPG
    d = NSA_HEAD_DIM
    wide = hpg * tq
    wb_refs, s_refs, mc_refs, st_refs, acc_refs = (scratch[n::5] for n in range(5))
    lane = lax.broadcasted_iota(jnp.int32, (tq, LANES), 1)
    slopes = [slope_ref[g * hpg + hh] for hh in range(hpg)]
    q0s = [(i * n_streams + n) * tq for n in range(n_streams)]

    def per_head_row(vals):
        return jnp.concatenate([jnp.full((1, tq), v, F32) for v in vals], axis=1)

    q4s = [jnp.concatenate(
        [_head_query_t(q_ref[n * tq:(n + 1) * tq, (hh // 2) * LANES:(hh // 2 + 1) * LANES], hh % 2, lane)
         for hh in range(hpg)], axis=1) for n in range(n_streams)]

    rel = (lax.broadcasted_iota(jnp.int32, (tk, tq), 1)
           - lax.broadcasted_iota(jnp.int32, (tk, tq), 0))

    @pl.when(i == 0)
    def _():
        rel_f = rel.astype(F32)
        sd_ref[...] = jnp.concatenate([rel_f * slopes[hh] for hh in range(hpg)], axis=1)

    bits = bits_ref[(b * pl.num_programs(1) + g) * pl.num_programs(2) + i]

    def collect(j, n):
        hit = lax.shift_right_logical(bits, j) & 1

        @pl.when(hit == 1)
        def _():
            list_ref[n] = j
        return n + hit

    n_causal = (q0s[-1] + tq + tk - 1) // tk
    n_act = lax.fori_loop(0, n_causal, collect, 0)

    for n in range(n_streams):
        st_refs[n][0:1, :] = jnp.full((1, wide), M_FLOOR, F32)
        st_refs[n][1:2, :] = jnp.zeros((1, wide), F32)
        acc_refs[n][...] = jnp.zeros(acc_refs[n].shape, F32)
    selts = [selt_ref[:, n * tq:(n + 1) * tq] for n in range(n_streams)]

    def scores(n, j):
        kv = kvs_ref[pl.ds(pl.multiple_of(j * tk, tk), tk), :]
        return _dot(kv, q4s[n])

    def produce(n, j, raw, slot):
        picked = _dot(eselt_ref[j], selts[n])
        causal = rel + (q0s[n] - j * tk) >= 0
        madd = jnp.where((picked > 0.5) & causal, 0.0, NEG_INF)
        s = raw - sd_ref[...] + jnp.concatenate([madd] * hpg, axis=1)
        s_refs[n][slot] = s
        mc_refs[n][slot, 0:1, :] = jnp.max(s, axis=0, keepdims=True)

    def consume(n, j, slot):
        off = per_head_row([slopes[hh] * (q0s[n] - j * tk).astype(F32) for hh in range(hpg)])
        st_ref, acc_ref = st_refs[n], acc_refs[n]
        m_prev = st_ref[0:1, :]
        l_prev = st_ref[1:2, :]
        m_next = jnp.maximum(m_prev, mc_refs[n][slot, 0:1, :] - off)
        alpha = jnp.exp(m_prev - m_next)
        p = jnp.exp(s_refs[n][slot] - (m_next + off))
        st_ref[0:1, :] = m_next
        st_ref[1:2, :] = alpha * l_prev + jnp.sum(p, axis=0, keepdims=True)
        vt = vst_ref[:, pl.ds(pl.multiple_of(j * tk, tk), tk)]
        acc_ref[...] = alpha * acc_ref[...] + _dot(vt, p.astype(BF16))

    j_first = list_ref[0]
    for n in range(n_streams):
        produce(n, j_first, scores(n, j_first), 0)

    def body(t, carry):
        j_new = list_ref[t]
        j_old = list_ref[t - 1]
        raws = [scores(n, j_new) for n in range(n_streams)]
        for n in range(n_streams):
            consume(n, j_old, (t - 1) & 1)
            produce(n, j_new, raws[n], t & 1)
        return carry

    lax.fori_loop(1, n_act, body, 0)
    j_last = list_ref[n_act - 1]

    for n in range(n_streams):
        q0 = q0s[n]
        consume(n, j_last, (n_act - 1) & 1)

        start = pl.multiple_of(jnp.maximum(q0 + tq - win_span, 0), tq)
        s_w = _dot(kvw_ref[pl.ds(start, win_span), :], q4s[n])

        @pl.when(q0 + tq - win_span < n_streams * tq)
        def _():
            dist_w = (lax.broadcasted_iota(jnp.int32, (win_span, tq), 1)
                      - lax.broadcasted_iota(jnp.int32, (win_span, tq), 0) + (q0 - start))
            valid_w = (dist_w >= 0) & (dist_w < NSA_WINDOW)
            dist_wf = dist_w.astype(F32)
            wb_refs[n][...] = jnp.concatenate(
                [jnp.where(valid_w, -slopes[hh] * dist_wf, NEG_INF) for hh in range(hpg)], axis=1)

        s_w = s_w + wb_refs[n][...]
        p_w = jnp.exp(s_w - jnp.max(s_w, axis=0, keepdims=True))
        l_w = jnp.sum(p_w, axis=0, keepdims=True)
        o_w = _dot(vwt_ref[:, pl.ds(start, win_span)], p_w.astype(BF16)) * (1.0 / l_w)

        l_s = st_refs[n][1:2, :]
        o_s = acc_refs[n][...] * jnp.where(l_s > 0.0, 1.0 / l_s, 0.0)
        gates_t = gates_ref[n * tq:(n + 1) * tq, :].T
        g_s = jnp.concatenate([gates_t[3 * hh + 1:3 * hh + 2] for hh in range(hpg)], axis=1)
        g_w = jnp.concatenate([gates_t[3 * hh + 2:3 * hh + 3] for hh in range(hpg)], axis=1)
        comb = g_s * o_s + g_w * o_w
        by_head = jnp.concatenate([comb[:, hh * tq:(hh + 1) * tq] for hh in range(hpg)], axis=0)
        o_ref[n * tq:(n + 1) * tq, :] = (by_head.T + oc_ref[n * tq:(n + 1) * tq, :]).astype(o_ref.dtype)


def _nsa_sel_win(bits, qkv, vst, vwt, selt, oc, gates, eselt, tq, tk):
    b, t, _ = qkv.shape
    g = NSA_GROUPS
    d = NSA_HEAD_DIM
    n_slc = t // NSA_SLC_LEN
    nq = t // tq
    hd = NSA_HPG * d
    wide = NSA_HPG * tq
    kv_col0 = NSA_HEADS * d // LANES
    win_span = min(NSA_WINDOW + tq, t)
    n_chunks = t // tk
    ns = SEL_STREAMS if nq % SEL_STREAMS == 0 else 1
    rows = ns * tq
    tile_bits = bits.reshape(-1, ns)
    bits = functools.reduce(jnp.bitwise_or, [tile_bits[:, n] for n in range(ns)])
    slopes = jnp.asarray(_alibi_slopes(NSA_HEADS))
    per_stream = [
        pltpu.VMEM((win_span, wide), F32),
        pltpu.VMEM((2, tk, wide), F32),
        pltpu.VMEM((2, 8, wide), F32),
        pltpu.VMEM((8, wide), F32),
        pltpu.VMEM((d, wide), F32),
    ]
    grid_spec = pltpu.PrefetchScalarGridSpec(
        num_scalar_prefetch=1,
        grid=(b, g, nq // ns),
        in_specs=[
            pl.BlockSpec(memory_space=pltpu.SMEM),
            pl.BlockSpec((None, rows, hd), lambda bb, gg, i, bits: (bb, i, gg)),
            pl.BlockSpec((None, t, LANES), lambda bb, gg, i, bits: (bb, 0, kv_col0 + gg)),
            pl.BlockSpec((None, d, t), lambda bb, gg, i, bits: (bb, gg, 0)),
            pl.BlockSpec((None, t, LANES), lambda bb, gg, i, bits: (bb, 0, kv_col0 + g + gg)),
            pl.BlockSpec((None, d, t), lambda bb, gg, i, bits: (bb, gg, 0)),
            pl.BlockSpec((None, None, n_slc, rows), lambda bb, gg, i, bits: (bb, gg, 0, i)),
            pl.BlockSpec((None, rows, hd), lambda bb, gg, i, bits: (bb, i, gg)),
            pl.BlockSpec((None, rows, LANES), lambda bb, gg, i, bits: (bb, i, gg)),
            pl.BlockSpec(eselt.shape, lambda bb, gg, i, bits: (0, 0, 0)),
        ],
        out_specs=pl.BlockSpec((None, rows, hd), lambda bb, gg, i, bits: (bb, i, gg)),
        scratch_shapes=[pltpu.SMEM((n_chunks,), jnp.int32), pltpu.VMEM((tk, wide), F32)] + per_stream * ns,
    )
    return pl.pallas_call(
        functools.partial(_nsa_sel_win_kernel, tq=tq, tk=tk, win_span=win_span, n_chunks=n_chunks,
                          n_streams=ns),
        out_shape=jax.ShapeDtypeStruct((b, t, NSA_HEADS * d), BF16),
        grid_spec=grid_spec,
        compiler_params=_cparams(("arbitrary", "arbitrary", "arbitrary")),
        name="nsa_sel_win",
    )(bits, slopes, qkv, qkv, vst, qkv, vwt, selt, oc, gates, eselt)


def _nsa_weight_layout():
    hd, g, d = NSA_HEADS * NSA_HEAD_DIM, NSA_GROUPS, NSA_HEAD_DIM
    base = {name: hd + n * g * d for n, name in enumerate(["kc", "vc", "ks", "vs", "kw", "vw"])}
    gate0 = hd + 6 * g * d
    cols = list(range(hd))
    for kn, vn in (("ks", "vs"), ("kw", "vw"), ("kc", "vc")):
        for gg in range(g):
            cols += list(range(base[kn] + gg * d, base[kn] + (gg + 1) * d))
            cols += list(range(base[vn] + gg * d, base[vn] + (gg + 1) * d))
    for gg in range(g):
        n_gate = NSA_HPG * 3
        cols += list(range(gate0 + gg * n_gate, gate0 + (gg + 1) * n_gate)) + [-1] * (LANES - n_gate)
    return np.asarray(cols, np.int32)


def _permute_columns(w, cols):
    picked = jnp.take(w, jnp.asarray(np.maximum(cols, 0)), axis=1)
    return jnp.where(jnp.asarray(cols >= 0)[None, :], picked, 0.0)


def _overlap_t(n_rows, n_slc):
    n_cmp = n_rows - 1
    c_start = np.arange(n_rows) * NSA_CMP_STRIDE
    s_start = np.arange(n_slc) * NSA_SLC_LEN
    ov = (np.minimum(c_start[:, None] + NSA_CMP_LEN, s_start[None, :] + NSA_SLC_LEN)
          - np.maximum(c_start[:, None], s_start[None, :]))
    ov = np.clip(ov, 0, None).astype(np.float32) / NSA_CMP_LEN
    ov[n_cmp:] = 0.0
    return jnp.asarray(ov.T, BF16)


def _block_expander(n_slc, tk):
    n_chunks = n_slc * NSA_SLC_LEN // tk
    key_blk = (np.arange(n_chunks)[:, None] * tk + np.arange(tk)[None, :]) // NSA_SLC_LEN
    e = (key_blk[:, :, None] == np.arange(n_slc)[None, None, :]).astype(np.float32)
    return jnp.asarray(e, BF16)


def _chunk_bits(any_sel, sub_per_tile, blocks_per_chunk):
    b, nq, _, n_slc = any_sel.shape
    g = NSA_GROUPS
    n_chunks = n_slc // blocks_per_chunk
    a = any_sel.reshape(b, nq, sub_per_tile, g, n_chunks, blocks_per_chunk).max(axis=-1) > 0.5
    a = a.transpose(0, 3, 1, 2, 4).reshape(b * g * nq * sub_per_tile, n_chunks)
    weights = jnp.left_shift(jnp.uint32(1), jnp.arange(n_chunks, dtype=jnp.uint32))
    return lax.bitcast_convert_type(jnp.sum(jnp.where(a, weights[None, :], jnp.uint32(0)), axis=1,
                                            dtype=jnp.uint32), jnp.int32)


def _diff_mixer(x, mod, layer, w_in, w_out, lam_q1, lam_k1, lam_q2, lam_k2, subln_g):
    n_qk = 2 * DIFF_HEADS * 2 * DIFF_HEAD_DIM
    n_v = DIFF_HEADS * DIFF_V_DIM
    w = w_in.astype(BF16)
    qk, vt = _modulated_projection(x, mod, layer, w[:, :n_qk], [(0, n_qk, "id")], [BF16],
                                   w[:, n_qk:].T, [(0, n_v)])
    lam_vecs = jnp.stack([lam_q1, lam_k1, lam_q2, lam_k2]).astype(F32)
    o = _diff_attention(qk, vt, lam_vecs, subln_g.astype(F32), layer)
    return o, w_out.astype(BF16)


def _nsa_mixer(x, mod, layer, w_in, pos_k, pos_v, k_w1, k_w2, v_w1, v_w2, w_out):
    b, t, _ = x.shape
    g, d = NSA_GROUPS, NSA_HEAD_DIM
    hd = NSA_HEADS * d
    cols = _nsa_weight_layout()
    w = _permute_columns(w_in, cols).astype(BF16)
    n_main = hd + 2 * (2 * g * d)
    n_cmp_cols = 2 * g * d
    plan = [(0, n_main, "id"), (n_main, n_main + n_cmp_cols, "id"),
            (n_main + n_cmp_cols, n_main + n_cmp_cols + g * LANES, "sigmoid")]
    v_cols = np.concatenate([np.arange(hd + 6 * g * d)[hd + (2 * n + 1) * g * d: hd + (2 * n + 2) * g * d]
                             for n in (1, 2)])
    wt = jnp.take(w_in, jnp.asarray(v_cols), axis=1).T.astype(BF16)
    qkv, kvc_tok, gates, vst, vwt = _modulated_projection(
        x, mod, layer, w, plan, [BF16, F32, F32], wt, [(0, g * d), (g * d, 2 * g * d)])

    rows = t // NSA_CMP_STRIDE
    pos = jnp.stack([pos_k.reshape(1, -1), pos_v.reshape(1, -1)]).astype(F32)
    w1 = jnp.stack([k_w1, v_w1]).astype(BF16)
    zeros = jnp.zeros((NSA_CMP_HIDDEN, d), F32)
    w2pad = jnp.stack([jnp.concatenate([k_w2, zeros], axis=1),
                       jnp.concatenate([zeros, v_w2], axis=1)]).astype(BF16)
    kvc, vct = _compress(kvc_tok, pos, w1, w2pad)

    n_slc = t // NSA_SLC_LEN
    tq_sel = min(SEL_Q_TILE, t)
    tq_cmp = min(CMP_Q_TILE, t)
    tk = min(SEL_K_TILE, t)
    oc, selt, any_sel = _nsa_cmp_topk(qkv, kvc, vct, gates, _overlap_t(rows, n_slc), tq_cmp, tq_sel)
    bits = _chunk_bits(any_sel, tq_cmp // tq_sel, tk // NSA_SLC_LEN)
    o = _nsa_sel_win(bits, qkv, vst, vwt, selt, oc, gates, _block_expander(n_slc, tk), tq_sel, tk)
    return o, w_out.astype(BF16)


def kernel(x, c, diff_w_in, diff_w_out, diff_lam_q1, diff_lam_k1, diff_lam_q2, diff_lam_k2, diff_subln_g,
           nsa_w_in, nsa_cmp_pos_k, nsa_cmp_pos_v, nsa_cmp_k_w1, nsa_cmp_k_w2, nsa_cmp_v_w1, nsa_cmp_v_w2,
           nsa_w_out, ada_w, ada_b, ln_mix_g, ln_mix_b, ln_ffn_g, ln_ffn_b, ffn_w_gu, ffn_w_down):
    mod = _ada_modulation(c, ada_w, ada_b)
    for i in range(DEPTH):
        j = i // N_MIXERS
        if i % N_MIXERS == 0:
            o, w_out = _diff_mixer(x, mod, i, diff_w_in[j], diff_w_out[j], diff_lam_q1[j], diff_lam_k1[j],
                                   diff_lam_q2[j], diff_lam_k2[j], diff_subln_g[j])
        else:
            o, w_out = _nsa_mixer(x, mod, i, nsa_w_in[j], nsa_cmp_pos_k[j], nsa_cmp_pos_v[j],
                                  nsa_cmp_k_w1[j], nsa_cmp_k_w2[j], nsa_cmp_v_w1[j], nsa_cmp_v_w2[j],
                                  nsa_w_out[j])
        x = _mix_ffn(o, w_out, x, mod, i, ln_mix_g[i], ln_mix_b[i],
                     ffn_w_gu[i].astype(BF16), ffn_w_down[i].astype(BF16), ln_ffn_g[i], ln_ffn_b[i])
    return x
```

```python
import functools
import math

import numpy as np
import jax
import jax.numpy as jnp
from jax import lax
from jax.experimental import pallas as pl
from jax.experimental.pallas import tpu as pltpu

F32 = jnp.float32
BF16 = jnp.bfloat16

D_MODEL = 1024
DEPTH = 2
N_MIXERS = 2
DEEPNORM_ALPHA = (2 * DEPTH) ** 0.25
LN_EPS = 1e-5
RMS_EPS = 1e-5
NEG_INF = -1e30
FORCED_SCORE = 1e6

DIFF_HEADS = 8
DIFF_HEAD_DIM = 64
DIFF_V_DIM = 2 * DIFF_HEAD_DIM

NSA_HEADS = 16
NSA_GROUPS = 4
NSA_HEAD_DIM = 64
NSA_HPG = NSA_HEADS // NSA_GROUPS
NSA_CMP_LEN = 32
NSA_CMP_STRIDE = 16
NSA_CMP_HIDDEN = 128
NSA_SLC_LEN = 64
NSA_SLC_TOPK = 16
NSA_WINDOW = 512

FFN_HIDDEN = -(-8 * D_MODEL // (3 * 256)) * 256

LANES = 128
VMEM_LIMIT = 56 * 1024 * 1024
PROJ_ROWS = 512
FFN_ROW_PARTS = 2
DIFF_TILE = 256
DIFF_STREAMS = 2
DIFF_ELIDE_LOG = 110.0
DIFF_ELIDE_SLACK = 1.02
CMP_Q_TILE = 256
SEL_Q_TILE = 128
SEL_K_TILE = 256
SEL_STREAMS = 2
M_FLOOR = -1e29
TAKEN = -3e38
ONES_ROWS = 16


def _cparams(sem):
    return pltpu.CompilerParams(dimension_semantics=sem, vmem_limit_bytes=VMEM_LIMIT)


def _dot(a, b):
    return jnp.dot(a, b, preferred_element_type=F32)


def _dot_nt(a, b):
    return lax.dot_general(a, b, (((1,), (1,)), ((), ())), preferred_element_type=F32)


def _layer_norm(z, g, b):
    mu = jnp.mean(z, axis=-1, keepdims=True)
    zc = z - mu
    var = jnp.mean(zc * zc, axis=-1, keepdims=True)
    return zc * lax.rsqrt(var + LN_EPS) * g + b


def _alibi_slopes(n_heads):
    return (2.0 ** (-8.0 * np.arange(1, n_heads + 1, dtype=np.float32) / n_heads)).astype(np.float32)


def _ada_kernel(c_ref, w_ref, b_ref, o_ref):
    c = c_ref[...]
    s = (c * jax.nn.sigmoid(c)).astype(BF16)
    o_ref[0] = _dot(s, w_ref[0].astype(BF16)) + b_ref[0]


def _ada_modulation(c, ada_w, ada_b):
    depth, d, d6 = ada_w.shape
    b = c.shape[0]
    rows = 8
    c_pad = jnp.zeros((rows, d), F32).at[:b].set(c)
    out = pl.pallas_call(
        _ada_kernel,
        out_shape=jax.ShapeDtypeStruct((depth, rows, d6), F32),
        grid=(depth, d6 // d),
        in_specs=[
            pl.BlockSpec((rows, d), lambda l, j: (0, 0)),
            pl.BlockSpec((1, d, d), lambda l, j: (l, 0, j)),
            pl.BlockSpec((1, 1, d), lambda l, j: (l, 0, j)),
        ],
        out_specs=pl.BlockSpec((1, rows, d), lambda l, j: (l, 0, j)),
        compiler_params=_cparams(("arbitrary", "arbitrary")),
        name="ada_modulation",
    )(c_pad, ada_w, ada_b.reshape(depth, 1, d6))
    return out[:, :b].reshape(depth, b, 6, 1, d)


def _mod_spec(layer, which, n_grid):
    d = D_MODEL
    if n_grid == 2:
        return pl.BlockSpec((None, None, None, 1, d), lambda b, i: (layer, b, which, 0, 0))
    return pl.BlockSpec((None, None, None, 1, d), lambda b, i, k: (layer, b, which, 0, 0))


def _proj_kernel(x_ref, sc_ref, sh_ref, w_ref, wt_ref, *out_refs, plan, t_plan):
    h = (x_ref[...] * (1.0 + sc_ref[...]) + sh_ref[...]).astype(BF16)
    for ref, (c0, c1, kind), in zip(out_refs, plan):
        step = 1024
        for s0 in range(c0, c1, step):
            s1 = min(s0 + step, c1)
            y = _dot(h, w_ref[:, s0:s1])
            if kind == "sigmoid":
                y = jax.nn.sigmoid(y)
            ref[:, s0 - c0:s1 - c0] = y.astype(ref.dtype)
    for ref, (r0, r1) in zip(out_refs[len(plan):], t_plan):
        ref[...] = _dot_nt(wt_ref[r0:r1, :], h).astype(ref.dtype)


def _modulated_projection(x, mod, layer, w, plan, out_dtypes, wt, t_plan):
    b, t, d = x.shape
    n = w.shape[1]
    tm = min(PROJ_ROWS, t)
    out_shape = [jax.ShapeDtypeStruct((b, t, c1 - c0), dt) for (c0, c1, _), dt in zip(plan, out_dtypes)]
    out_specs = [pl.BlockSpec((None, tm, c1 - c0), lambda bb, i: (bb, i, 0)) for (c0, c1, _) in plan]
    out_shape += [jax.ShapeDtypeStruct((b, r1 - r0, t), BF16) for (r0, r1) in t_plan]
    out_specs += [pl.BlockSpec((None, r1 - r0, tm), lambda bb, i: (bb, 0, i)) for (r0, r1) in t_plan]
    return pl.pallas_call(
        functools.partial(_proj_kernel, plan=plan, t_plan=t_plan),
        out_shape=out_shape,
        grid=(b, t // tm),
        in_specs=[
            pl.BlockSpec((None, tm, d), lambda bb, i: (bb, i, 0)),
            _mod_spec(layer, 1, 2),
            _mod_spec(layer, 0, 2),
            pl.BlockSpec((d, n), lambda bb, i: (0, 0)),
            pl.BlockSpec(wt.shape, lambda bb, i: (0, 0)),
        ],
        out_specs=out_specs,
        compiler_params=_cparams(("arbitrary", "arbitrary")),
        name=f"mod_proj_l{layer}",
    )(x, mod, mod, w, wt)


def _diff_attn_kernel(slope_ref, lam_ref, g_ref, q_ref, k_ref, vt_ref, o_ref, kn_ref, *scratch,
                      tile, lam_init, n_streams):
    h = pl.program_id(1)
    qi = pl.program_id(2)
    slope = slope_ref[0, h]
    inv_slope = slope_ref[1, h]
    half = DIFF_HEAD_DIM
    s_refs, mc_refs, st_refs, acc_refs = (scratch[n::4] for n in range(4))

    key = lax.broadcasted_iota(jnp.int32, (tile, tile), 0)
    qry = lax.broadcasted_iota(jnp.int32, (tile, tile), 1)
    lane = lax.broadcasted_iota(jnp.int32, (tile, LANES), 1)
    key_pos = lax.broadcasted_iota(jnp.int32, (tile, LANES), 0).astype(F32) * slope
    k_extra = jnp.where(lane == 0, 1.0, jnp.where(lane == 1, key_pos, 0.0)).astype(BF16)
    row = lax.broadcasted_iota(jnp.int32, (LANES, tile), 0)
    qry_pos = lax.broadcasted_iota(jnp.int32, (LANES, tile), 1).astype(F32) * slope
    q_extra = jnp.where(row == 0, -qry_pos, jnp.where(row == 1, 1.0, 0.0)).astype(BF16)
    ones_rows = jnp.ones((ONES_ROWS, tile), BF16)

    def max_component_norm(x, lane_ids):
        sq = x * x
        lo = jnp.sum(jnp.where(lane_ids < half, sq, 0.0), axis=1, keepdims=True)
        hi = jnp.sum(jnp.where(lane_ids >= half, sq, 0.0), axis=1, keepdims=True)
        return jnp.max(jnp.sqrt(jnp.max(jnp.maximum(lo, hi), axis=0, keepdims=True)))

    @pl.when(qi == 0)
    def _():
        for n in range(n_streams):
            kf = k_ref[n].astype(F32)
            kn_ref[n] = max_component_norm(kf, lax.broadcasted_iota(jnp.int32, kf.shape, 1))

    qcs = []
    reach_max = jnp.float32(0.0)
    for n in range(n_streams):
        qf = q_ref[n].astype(F32) * (DIFF_HEAD_DIM ** -0.5)
        qcs.append(tuple(jnp.concatenate([jnp.where(sel, qf, 0.0).T.astype(BF16), q_extra], axis=0)
                         for sel in (lane < half, lane >= half)))
        qn = max_component_norm(qf, lane)
        reach = (DIFF_ELIDE_LOG + DIFF_ELIDE_SLACK * 2.0 * qn * kn_ref[n]) * inv_slope
        reach_max = jnp.maximum(reach_max, reach)
        acc_refs[n][...] = jnp.zeros(acc_refs[n].shape, F32)
        st_refs[n][...] = jnp.full(st_refs[n].shape, M_FLOOR, F32)
    n_keep = lax.fori_loop(
        0, qi, lambda a, cnt: cnt + jnp.where((a * tile + 1).astype(F32) < reach_max, 1, 0), 0)
    j0 = qi - n_keep

    def scores(n, j):
        k = k_ref[n, pl.ds(pl.multiple_of(j * tile, tile), tile), :]
        k = jnp.concatenate([k, k_extra], axis=1)
        return [_dot(k, qcs[n][c]) for c in range(2)]

    def produce(n, j, raw, diagonal):
        slot = j & 1
        for c in range(2):
            s = raw[c]
            if diagonal is True:
                s = jnp.where(key <= qry, s, NEG_INF)
            elif diagonal is not False:
                s = jnp.where((key <= qry) | jnp.logical_not(diagonal), s, NEG_INF)
            s_refs[n][slot, c] = s
            mc_refs[n][slot, c:c + 1, :] = jnp.max(s, axis=0, keepdims=True)

    def consume(n, j):
        slot = j & 1
        vt = vt_ref[n, :, pl.ds(pl.multiple_of(j * tile, tile), tile)]
        vt = jnp.concatenate([vt, ones_rows], axis=0)
        off = slope * ((qi - j) * tile).astype(F32)
        st_ref, acc_ref = st_refs[n], acc_refs[n]
        for c in range(2):
            m_prev = st_ref[c:c + 1, :]
            m_next = jnp.maximum(m_prev, mc_refs[n][slot, c:c + 1, :] - off)
            alpha = jnp.exp(m_prev - m_next)
            p = jnp.exp(s_refs[n][slot, c] - (m_next + off))
            st_ref[c:c + 1, :] = m_next
            acc_ref[c] = alpha * acc_ref[c] + _dot(vt, p.astype(BF16))

    for n in range(n_streams):
        produce(n, j0, scores(n, j0), j0 == qi)

    def step(t, diagonal):
        raws = [scores(n, t) for n in range(n_streams)]
        for n in range(n_streams):
            consume(n, t - 1)
            produce(n, t, raws[n], diagonal)

    def body(t, carry):
        step(t, False)
        return carry

    lax.fori_loop(j0 + 1, qi, body, 0)

    @pl.when(qi > j0)
    def _():
        step(qi, True)

    lam_v = lam_ref[...]
    lam = (jnp.exp(jnp.sum(lam_v[0:1] * lam_v[1:2], axis=1, keepdims=True))
           - jnp.exp(jnp.sum(lam_v[2:3] * lam_v[3:4], axis=1, keepdims=True)) + lam_init)
    for n in range(n_streams):
        consume(n, qi)
        dv = DIFF_V_DIM
        l0 = acc_refs[n][0, dv:dv + 1, :]
        l1 = acc_refs[n][1, dv:dv + 1, :]
        o = acc_refs[n][0, :dv, :] / l0 - lam * (acc_refs[n][1, :dv, :] / l1)
        o = o * lax.rsqrt(jnp.mean(o * o, axis=0, keepdims=True) + RMS_EPS)
        o_ref[n] = ((o * g_ref[...] * (1.0 - lam_init)).T).astype(o_ref.dtype)


def _diff_attention(qk, vt, lam_vecs, subln_g, layer_idx):
    b, t, _ = qk.shape
    hh = DIFF_HEADS
    tile = min(DIFF_TILE, t)
    ns = DIFF_STREAMS if b % DIFF_STREAMS == 0 else 1
    lam_init = 0.8 - 0.6 * math.exp(-0.3 * layer_idx)
    slopes = _alibi_slopes(hh)
    slope_tab = jnp.asarray(np.stack([slopes, 1.0 / slopes]))
    per_stream = [
        pltpu.VMEM((2, 2, tile, tile), F32),
        pltpu.VMEM((2, 8, tile), F32),
        pltpu.VMEM((8, tile), F32),
        pltpu.VMEM((2, DIFF_V_DIM + ONES_ROWS, tile), F32),
    ]
    col = (slopes[:, None] * np.arange(tile, dtype=np.float32)[None, :]).astype(np.float32)
    assert np.array_equal(col.astype(jnp.bfloat16).astype(np.float32), col)
    return pl.pallas_call(
        functools.partial(_diff_attn_kernel, tile=tile, lam_init=lam_init, n_streams=ns),
        out_shape=jax.ShapeDtypeStruct((b, t, hh * DIFF_V_DIM), BF16),
        grid=(b // ns, hh, t // tile),
        in_specs=[
            pl.BlockSpec(memory_space=pltpu.SMEM),
            pl.BlockSpec((4, DIFF_HEAD_DIM), lambda bb, h, i: (0, 0)),
            pl.BlockSpec((DIFF_V_DIM, 1), lambda bb, h, i: (0, 0)),
            pl.BlockSpec((ns, tile, LANES), lambda bb, h, i: (bb, i, h)),
            pl.BlockSpec((ns, t, LANES), lambda bb, h, i: (bb, 0, hh + h)),
            pl.BlockSpec((ns, DIFF_V_DIM, t), lambda bb, h, i: (bb, h, 0)),
        ],
        out_specs=pl.BlockSpec((ns, tile, LANES), lambda bb, h, i: (bb, i, h)),
        scratch_shapes=[pltpu.SMEM((ns,), F32)] + per_stream * ns,
        compiler_params=_cparams(("arbitrary", "arbitrary", "arbitrary")),
        name="diff_attention",
    )(slope_tab, lam_vecs, subln_g.reshape(DIFF_V_DIM, 1), qk, qk, vt)


def _mix_ffn_kernel(o_ref, wo_ref, x_ref, ga_ref, lng_a_ref, lnb_a_ref, sc_ref, sh_ref, gf_ref,
                    wgu_ref, wd_ref, lng_f_ref, lnb_f_ref, out_ref, *, n_parts):
    rows = o_ref.shape[0] // n_parts
    fh = wd_ref.shape[0]
    for r in range(n_parts):
        rs = slice(r * rows, (r + 1) * rows)
        y = _dot(o_ref[rs, :], wo_ref[...])
        z = DEEPNORM_ALPHA * x_ref[rs, :] + (1.0 + ga_ref[...]) * y
        x1 = _layer_norm(z, lng_a_ref[...], lnb_a_ref[...])
        h = (x1 * (1.0 + sc_ref[...]) + sh_ref[...]).astype(BF16)
        gp = _dot(h, wgu_ref[:, :fh])
        up = _dot(h, wgu_ref[:, fh:])
        act = (gp * jax.nn.sigmoid(gp) * up).astype(BF16)
        z = DEEPNORM_ALPHA * x1 + (1.0 + gf_ref[...]) * _dot(act, wd_ref[...])
        out_ref[rs, :] = _layer_norm(z, lng_f_ref[...], lnb_f_ref[...])


def _mix_ffn(o, w_out, x, mod, layer, ln_mix_g, ln_mix_b, w_gu, w_down, ln_ffn_g, ln_ffn_b):
    b, t, d = x.shape
    dh = o.shape[-1]
    tm = min(PROJ_ROWS, t)
    once = pl.Buffered(1)
    row_vec = pl.BlockSpec((1, d), lambda bb, i: (0, 0))
    return pl.pallas_call(
        functools.partial(_mix_ffn_kernel, n_parts=FFN_ROW_PARTS),
        out_shape=jax.ShapeDtypeStruct((b, t, d), F32),
        grid=(b, t // tm),
        in_specs=[
            pl.BlockSpec((None, tm, dh), lambda bb, i: (bb, i, 0)),
            pl.BlockSpec((dh, d), lambda bb, i: (0, 0), pipeline_mode=once),
            pl.BlockSpec((None, tm, d), lambda bb, i: (bb, i, 0)),
            _mod_spec(layer, 2, 2),
            row_vec,
            row_vec,
            _mod_spec(layer, 4, 2),
            _mod_spec(layer, 3, 2),
            _mod_spec(layer, 5, 2),
            pl.BlockSpec(w_gu.shape, lambda bb, i: (0, 0), pipeline_mode=once),
            pl.BlockSpec(w_down.shape, lambda bb, i: (0, 0), pipeline_mode=once),
            row_vec,
            row_vec,
        ],
        out_specs=pl.BlockSpec((None, tm, d), lambda bb, i: (bb, i, 0)),
        compiler_params=_cparams(("arbitrary", "arbitrary")),
        name=f"mix_ffn_l{layer}",
    )(o, w_out, x, mod, ln_mix_g.reshape(1, d), ln_mix_b.reshape(1, d), mod, mod, mod,
      w_gu, w_down, ln_ffn_g.reshape(1, d), ln_ffn_b.reshape(1, d))


def _gelu_tanh(x):
    return 0.5 * x * (1.0 + jnp.tanh(math.sqrt(2.0 / math.pi) * (x + 0.044715 * (x * x * x))))


def _compress_kernel(x_ref, pos_ref, w1_ref, w2_ref, o_ref, vt_ref):
    d = NSA_HEAD_DIM
    n_rows, stride, _ = x_ref.shape
    half = stride * d
    lane = lax.broadcasted_iota(jnp.int32, (n_rows, LANES), 1)
    pieces = ([], [])
    for lp in range(stride // 2):
        even = x_ref[:, 2 * lp, :]
        odd = x_ref[:, 2 * lp + 1, :]
        pieces[0].append(jnp.where(lane < d, even, pltpu.roll(odd, d, 1)))
        pieces[1].append(jnp.where(lane < d, pltpu.roll(even, d, 1), odd))
    out = None
    for kv in range(2):
        r = jnp.concatenate(pieces[kv], axis=1)
        r_next = pltpu.roll(r, n_rows - 1, 0)
        a = (r + pos_ref[kv, :, :half]).astype(BF16)
        bnx = (r_next + pos_ref[kv, :, half:]).astype(BF16)
        hid = _dot(a, w1_ref[kv, :half, :]) + _dot(bnx, w1_ref[kv, half:, :])
        y = _dot(_gelu_tanh(hid).astype(BF16), w2_ref[kv])
        out = y if out is None else out + y
    o_ref[...] = out.astype(o_ref.dtype)
    vt_ref[...] = out.T[NSA_HEAD_DIM:, :].astype(vt_ref.dtype)


def _compress(kvc_tok, pos, w1, w2pad):
    b, t, _ = kvc_tok.shape
    g = NSA_GROUPS
    rows = t // NSA_CMP_STRIDE
    width = NSA_CMP_STRIDE * NSA_HEAD_DIM
    return pl.pallas_call(
        _compress_kernel,
        out_shape=[jax.ShapeDtypeStruct((b, g, rows, LANES), BF16),
                   jax.ShapeDtypeStruct((b, g, NSA_HEAD_DIM, rows), BF16)],
        grid=(b, g),
        in_specs=[
            pl.BlockSpec((None, rows, NSA_CMP_STRIDE, LANES), lambda bb, gg: (bb, 0, 0, gg)),
            pl.BlockSpec((2, 1, 2 * width), lambda bb, gg: (0, 0, 0)),
            pl.BlockSpec((2, 2 * width, NSA_CMP_HIDDEN), lambda bb, gg: (0, 0, 0)),
            pl.BlockSpec((2, NSA_CMP_HIDDEN, LANES), lambda bb, gg: (0, 0, 0)),
        ],
        out_specs=[pl.BlockSpec((None, None, rows, LANES), lambda bb, gg: (bb, gg, 0, 0)),
                   pl.BlockSpec((None, None, NSA_HEAD_DIM, rows), lambda bb, gg: (bb, gg, 0, 0))],
        compiler_params=_cparams(("arbitrary", "arbitrary")),
        name="nsa_compress",
    )(kvc_tok.reshape(b, rows, NSA_CMP_STRIDE, g * LANES), pos, w1, w2pad)


def _head_query_t(q_pair, head_in_pair, lane):
    d = NSA_HEAD_DIM
    qf = q_pair.astype(F32) * (d ** -0.5)
    if head_in_pair == 1:
        qf = pltpu.roll(qf, d, 1)
    return jnp.where(lane < d, qf, 0.0).T.astype(BF16)


def _merge_head_pair(o_even, o_odd, lane):
    return jnp.where(lane < NSA_HEAD_DIM, pltpu.roll(o_even, NSA_HEAD_DIM, 1), o_odd)


def _nsa_cmp_topk_kernel(slope_ref, q_ref, kvc_ref, vct_ref, gates_ref, ovt_ref,
                         oc_ref, sel_ref, any_ref, sdc_ref, *, tq, n_rows, n_slc, k_sel, sub_tile):
    i = pl.program_id(1)
    q0 = i * tq
    hpg = NSA_HPG
    d = NSA_HEAD_DIM
    lane = lax.broadcasted_iota(jnp.int32, (tq, LANES), 1)
    qry_c = lax.broadcasted_iota(jnp.int32, (n_rows, tq), 1)
    end_c = lax.broadcasted_iota(jnp.int32, (n_rows, tq), 0) * NSA_CMP_STRIDE + (NSA_CMP_LEN - 1)

    @pl.when(i == 0)
    def _():
        rel_c = (qry_c - end_c).astype(F32)
        for h in range(NSA_HEADS):
            sdc_ref[:, h * tq:(h + 1) * tq] = rel_c * slope_ref[h]

    valid = end_c <= q0 + qry_c

    blk_i = lax.broadcasted_iota(jnp.int32, (n_slc, tq), 0)
    qpos = q0 + lax.broadcasted_iota(jnp.int32, (n_slc, tq), 1)
    q_blk = lax.shift_right_logical(qpos, int(math.log2(NSA_SLC_LEN)))
    forced = (blk_i == 0) | (blk_i == q_blk) | (blk_i == q_blk - 1)
    causal_blk = blk_i * NSA_SLC_LEN <= qpos
    blk_f = blk_i.astype(F32)

    for g in range(NSA_GROUPS):
        q4t = jnp.concatenate(
            [_head_query_t(q_ref[:, ((g * hpg + hh) // 2) * LANES:((g * hpg + hh) // 2 + 1) * LANES], hh % 2, lane)
             for hh in range(hpg)], axis=1)
        raw = _dot(kvc_ref[g], q4t)
        s = jnp.concatenate(
            [jnp.where(valid, raw[:, hh * tq:(hh + 1) * tq]
                       - sdc_ref[:, (g * hpg + hh) * tq:(g * hpg + hh + 1) * tq], NEG_INF)
             for hh in range(hpg)], axis=1)
        m = jnp.max(s, axis=0, keepdims=True)
        p = jnp.exp(s - m)
        l = jnp.sum(p, axis=0, keepdims=True)
        inv = jnp.where(m > 0.5 * NEG_INF, 1.0 / l, 0.0)
        lhs = jnp.concatenate([vct_ref[g], ovt_ref[...]], axis=0)
        r = _dot(lhs, p.astype(BF16))
        gates_t = gates_ref[:, g * LANES:(g + 1) * LANES].T
        w = inv * jnp.concatenate([gates_t[3 * hh:3 * hh + 1] for hh in range(hpg)], axis=1)
        oc_t = r[:d] * w
        by_head = jnp.concatenate([oc_t[:, hh * tq:(hh + 1) * tq] for hh in range(hpg)], axis=0)
        oc_ref[:, g * hpg * d:(g + 1) * hpg * d] = by_head.T
        imp_t = r[d:, 0:tq] * inv[:, 0:tq]
        for hh in range(1, hpg):
            imp_t = imp_t + r[d:, hh * tq:(hh + 1) * tq] * inv[:, hh * tq:(hh + 1) * tq]

        imp = jnp.where(causal_blk, jnp.where(forced, FORCED_SCORE, imp_t), NEG_INF)
        for _ in range(k_sel):
            mx = jnp.max(imp, axis=0, keepdims=True)
            cand = jnp.where(imp == mx, blk_f, float(n_slc))
            pick = blk_f == jnp.min(cand, axis=0, keepdims=True)
            imp = jnp.where(pick, TAKEN, imp)
        sel_b = jnp.where(causal_blk & (imp < 0.5 * TAKEN), 1.0, 0.0).astype(BF16)
        sel_ref[g] = sel_b
        ones = jnp.ones((8, sub_tile), BF16)
        for sub in range(tq // sub_tile):
            r = sub * NSA_GROUPS + g
            cnt = _dot_nt(ones, sel_b[:, sub * sub_tile:(sub + 1) * sub_tile])
            any_ref[r:r + 1, :] = cnt[0:1]


def _nsa_cmp_topk(qkv, kvc, vct, gates, ovt, tq, sub_tile):
    b, t, _ = qkv.shape
    g = NSA_GROUPS
    n_rows = kvc.shape[2]
    n_slc = t // NSA_SLC_LEN
    k_sel = min(NSA_SLC_TOPK, n_slc)
    nq = t // tq
    assert n_slc == LANES and (tq // sub_tile) * g == 8
    slopes = jnp.asarray(_alibi_slopes(NSA_HEADS))
    return pl.pallas_call(
        functools.partial(_nsa_cmp_topk_kernel, tq=tq, n_rows=n_rows, n_slc=n_slc, k_sel=k_sel,
                          sub_tile=sub_tile),
        out_shape=[
            jax.ShapeDtypeStruct((b, t, NSA_HEADS * NSA_HEAD_DIM), F32),
            jax.ShapeDtypeStruct((b, g, n_slc, t), BF16),
            jax.ShapeDtypeStruct((b, nq, 8, n_slc), F32),
        ],
        grid=(b, nq),
        in_specs=[
            pl.BlockSpec(memory_space=pltpu.SMEM),
            pl.BlockSpec((None, tq, NSA_HEADS * NSA_HEAD_DIM), lambda bb, i: (bb, i, 0)),
            pl.BlockSpec((None, g, n_rows, LANES), lambda bb, i: (bb, 0, 0, 0)),
            pl.BlockSpec((None, g, NSA_HEAD_DIM, n_rows), lambda bb, i: (bb, 0, 0, 0)),
            pl.BlockSpec((None, tq, g * LANES), lambda bb, i: (bb, i, 0)),
            pl.BlockSpec((n_slc, n_rows), lambda bb, i: (0, 0)),
        ],
        out_specs=[
            pl.BlockSpec((None, tq, NSA_HEADS * NSA_HEAD_DIM), lambda bb, i: (bb, i, 0)),
            pl.BlockSpec((None, g, n_slc, tq), lambda bb, i: (bb, 0, 0, i)),
            pl.BlockSpec((None, None, 8, n_slc), lambda bb, i: (bb, i, 0, 0)),
        ],
        scratch_shapes=[pltpu.VMEM((n_rows, NSA_HEADS * tq), F32)],
        compiler_params=_cparams(("arbitrary", "arbitrary")),
        name="nsa_cmp_topk",
    )(slopes, qkv, kvc, vct, gates, ovt)


def _nsa_sel_win_kernel(bits_ref, slope_ref, q_ref, kvs_ref, vst_ref, kvw_ref, vwt_ref, selt_ref, oc_ref,
                        gates_ref, eselt_ref, o_ref, list_ref, sd_ref, *scratch,
                        tq, tk, win_span, n_chunks, n_streams):
    b = pl.program_id(0)
    g = pl.program_id(1)
    i = pl.program_id(2)
    hpg = NSA_HPG
    d = NSA_HEAD_DIM
    wide = hpg * tq
    wb_refs, s_refs, mc_refs, st_refs, acc_refs = (scratch[n::5] for n in range(5))
    lane = lax.broadcasted_iota(jnp.int32, (tq, LANES), 1)
    slopes = [slope_ref[g * hpg + hh] for hh in range(hpg)]
    q0s = [(i * n_streams + n) * tq for n in range(n_streams)]

    def per_head_row(vals):
        return jnp.concatenate([jnp.full((1, tq), v, F32) for v in vals], axis=1)

    q4s = [jnp.concatenate(
        [_head_query_t(q_ref[n * tq:(n + 1) * tq, (hh // 2) * LANES:(hh // 2 + 1) * LANES], hh % 2, lane)
         for hh in range(hpg)], axis=1) for n in range(n_streams)]

    rel = (lax.broadcasted_iota(jnp.int32, (tk, tq), 1)
           - lax.broadcasted_iota(jnp.int32, (tk, tq), 0))

    @pl.when(i == 0)
    def _():
        rel_f = rel.astype(F32)
        sd_ref[...] = jnp.concatenate([rel_f * slopes[hh] for hh in range(hpg)], axis=1)

    bits = bits_ref[(b * pl.num_programs(1) + g) * pl.num_programs(2) + i]

    def collect(j, n):
        hit = lax.shift_right_logical(bits, j) & 1

        @pl.when(hit == 1)
        def _():
            list_ref[n] = j
        return n + hit

    n_causal = (q0s[-1] + tq + tk - 1) // tk
    n_act = lax.fori_loop(0, n_causal, collect, 0)

    for n in range(n_streams):
        st_refs[n][0:1, :] = jnp.full((1, wide), M_FLOOR, F32)
        st_refs[n][1:2, :] = jnp.zeros((1, wide), F32)
        acc_refs[n][...] = jnp.zeros(acc_refs[n].shape, F32)
    selts = [selt_ref[:, n * tq:(n + 1) * tq] for n in range(n_streams)]

    def scores(n, j):
        kv = kvs_ref[pl.ds(pl.multiple_of(j * tk, tk), tk), :]
        return _dot(kv, q4s[n])

    def produce(n, j, raw, slot):
        picked = _dot(eselt_ref[j], selts[n])
        causal = rel + (q0s[n] - j * tk) >= 0
        madd = jnp.where((picked > 0.5) & causal, 0.0, NEG_INF)
        s = raw - sd_ref[...] + jnp.concatenate([madd] * hpg, axis=1)
        s_refs[n][slot] = s
        mc_refs[n][slot, 0:1, :] = jnp.max(s, axis=0, keepdims=True)

    def consume(n, j, slot):
        off = per_head_row([slopes[hh] * (q0s[n] - j * tk).astype(F32) for hh in range(hpg)])
        st_ref, acc_ref = st_refs[n], acc_refs[n]
        m_prev = st_ref[0:1, :]
        l_prev = st_ref[1:2, :]
        m_next = jnp.maximum(m_prev, mc_refs[n][slot, 0:1, :] - off)
        alpha = jnp.exp(m_prev - m_next)
        p = jnp.exp(s_refs[n][slot] - (m_next + off))
        st_ref[0:1, :] = m_next
        st_ref[1:2, :] = alpha * l_prev + jnp.sum(p, axis=0, keepdims=True)
        vt = vst_ref[:, pl.ds(pl.multiple_of(j * tk, tk), tk)]
        acc_ref[...] = alpha * acc_ref[...] + _dot(vt, p.astype(BF16))

    j_first = list_ref[0]
    for n in range(n_streams):
        produce(n, j_first, scores(n, j_first), 0)

    def body(t, carry):
        j_new = list_ref[t]
        j_old = list_ref[t - 1]
        raws = [scores(n, j_new) for n in range(n_streams)]
        for n in range(n_streams):
            consume(n, j_old, (t - 1) & 1)
            produce(n, j_new, raws[n], t & 1)
        return carry

    lax.fori_loop(1, n_act, body, 0)
    j_last = list_ref[n_act - 1]

    for n in range(n_streams):
        q0 = q0s[n]
        consume(n, j_last, (n_act - 1) & 1)

        start = pl.multiple_of(jnp.maximum(q0 + tq - win_span, 0), tq)
        s_w = _dot(kvw_ref[pl.ds(start, win_span), :], q4s[n])

        @pl.when(q0 + tq - win_span < n_streams * tq)
        def _():
            dist_w = (lax.broadcasted_iota(jnp.int32, (win_span, tq), 1)
                      - lax.broadcasted_iota(jnp.int32, (win_span, tq), 0) + (q0 - start))
            valid_w = (dist_w >= 0) & (dist_w < NSA_WINDOW)
            dist_wf = dist_w.astype(F32)
            wb_refs[n][...] = jnp.concatenate(
                [jnp.where(valid_w, -slopes[hh] * dist_wf, NEG_INF) for hh in range(hpg)], axis=1)

        s_w = s_w + wb_refs[n][...]
        p_w = jnp.exp(s_w - jnp.max(s_w, axis=0, keepdims=True))
        l_w = jnp.sum(p_w, axis=0, keepdims=True)
        o_w = _dot(vwt_ref[:, pl.ds(start, win_span)], p_w.astype(BF16)) * (1.0 / l_w)

        l_s = st_refs[n][1:2, :]
        o_s = acc_refs[n][...] * jnp.where(l_s > 0.0, 1.0 / l_s, 0.0)
        gates_t = gates_ref[n * tq:(n + 1) * tq, :].T
        g_s = jnp.concatenate([gates_t[3 * hh + 1:3 * hh + 2] for hh in range(hpg)], axis=1)
        g_w = jnp.concatenate([gates_t[3 * hh + 2:3 * hh + 3] for hh in range(hpg)], axis=1)
        comb = g_s * o_s + g_w * o_w
        by_head = jnp.concatenate([comb[:, hh * tq:(hh + 1) * tq] for hh in range(hpg)], axis=0)
        o_ref[n * tq:(n + 1) * tq, :] = (by_head.T + oc_ref[n * tq:(n + 1) * tq, :]).astype(o_ref.dtype)


def _nsa_sel_win(bits, qkv, vst, vwt, selt, oc, gates, eselt, tq, tk):
    b, t, _ = qkv.shape
    g = NSA_GROUPS
    d = NSA_HEAD_DIM
    n_slc = t // NSA_SLC_LEN
    nq = t // tq
    hd = NSA_HPG * d
    wide = NSA_HPG * tq
    kv_col0 = NSA_HEADS * d // LANES
    win_span = min(NSA_WINDOW + tq, t)
    n_chunks = t // tk
    ns = SEL_STREAMS if nq % SEL_STREAMS == 0 else 1
    rows = ns * tq
    tile_bits = bits.reshape(-1, ns)
    bits = functools.reduce(jnp.bitwise_or, [tile_bits[:, n] for n in range(ns)])
    slopes = jnp.asarray(_alibi_slopes(NSA_HEADS))
    per_stream = [
        pltpu.VMEM((win_span, wide), F32),
        pltpu.VMEM((2, tk, wide), F32),
        pltpu.VMEM((2, 8, wide), F32),
        pltpu.VMEM((8, wide), F32),
        pltpu.VMEM((d, wide), F32),
    ]
    grid_spec = pltpu.PrefetchScalarGridSpec(
        num_scalar_prefetch=1,
        grid=(b, g, nq // ns),
        in_specs=[
            pl.BlockSpec(memory_space=pltpu.SMEM),
            pl.BlockSpec((None, rows, hd), lambda bb, gg, i, bits: (bb, i, gg)),
            pl.BlockSpec((None, t, LANES), lambda bb, gg, i, bits: (bb, 0, kv_col0 + gg)),
            pl.BlockSpec((None, d, t), lambda bb, gg, i, bits: (bb, gg, 0)),
            pl.BlockSpec((None, t, LANES), lambda bb, gg, i, bits: (bb, 0, kv_col0 + g + gg)),
            pl.BlockSpec((None, d, t), lambda bb, gg, i, bits: (bb, gg, 0)),
            pl.BlockSpec((None, None, n_slc, rows), lambda bb, gg, i, bits: (bb, gg, 0, i)),
            pl.BlockSpec((None, rows, hd), lambda bb, gg, i, bits: (bb, i, gg)),
            pl.BlockSpec((None, rows, LANES), lambda bb, gg, i, bits: (bb, i, gg)),
            pl.BlockSpec(eselt.shape, lambda bb, gg, i, bits: (0, 0, 0)),
        ],
        out_specs=pl.BlockSpec((None, rows, hd), lambda bb, gg, i, bits: (bb, i, gg)),
        scratch_shapes=[pltpu.SMEM((n_chunks,), jnp.int32), pltpu.VMEM((tk, wide), F32)] + per_stream * ns,
    )
    return pl.pallas_call(
        functools.partial(_nsa_sel_win_kernel, tq=tq, tk=tk, win_span=win_span, n_chunks=n_chunks,
                          n_streams=ns),
        out_shape=jax.ShapeDtypeStruct((b, t, NSA_HEADS * d), BF16),
        grid_spec=grid_spec,
        compiler_params=_cparams(("arbitrary", "arbitrary", "arbitrary")),
        name="nsa_sel_win",
    )(bits, slopes, qkv, qkv, vst, qkv, vwt, selt, oc, gates, eselt)


def _nsa_weight_layout():
    hd, g, d = NSA_HEADS * NSA_HEAD_DIM, NSA_GROUPS, NSA_HEAD_DIM
    base = {name: hd + n * g * d for n, name in enumerate(["kc", "vc", "ks", "vs", "kw", "vw"])}
    gate0 = hd + 6 * g * d
    cols = list(range(hd))
    for kn, vn in (("ks", "vs"), ("kw", "vw"), ("kc", "vc")):
        for gg in range(g):
            cols += list(range(base[kn] + gg * d, base[kn] + (gg + 1) * d))
            cols += list(range(base[vn] + gg * d, base[vn] + (gg + 1) * d))
    for gg in range(g):
        n_gate = NSA_HPG * 3
        cols += list(range(gate0 + gg * n_gate, gate0 + (gg + 1) * n_gate)) + [-1] * (LANES - n_gate)
    return np.asarray(cols, np.int32)


def _permute_columns(w, cols):
    picked = jnp.take(w, jnp.asarray(np.maximum(cols, 0)), axis=1)
    return jnp.where(jnp.asarray(cols >= 0)[None, :], picked, 0.0)


def _overlap_t(n_rows, n_slc):
    n_cmp = n_rows - 1
    c_start = np.arange(n_rows) * NSA_CMP_STRIDE
    s_start = np.arange(n_slc) * NSA_SLC_LEN
    ov = (np.minimum(c_start[:, None] + NSA_CMP_LEN, s_start[None, :] + NSA_SLC_LEN)
          - np.maximum(c_start[:, None], s_start[None, :]))
    ov = np.clip(ov, 0, None).astype(np.float32) / NSA_CMP_LEN
    ov[n_cmp:] = 0.0
    return jnp.asarray(ov.T, BF16)


def _block_expander(n_slc, tk):
    n_chunks = n_slc * NSA_SLC_LEN // tk
    key_blk = (np.arange(n_chunks)[:, None] * tk + np.arange(tk)[None, :]) // NSA_SLC_LEN
    e = (key_blk[:, :, None] == np.arange(n_slc)[None, None, :]).astype(np.float32)
    return jnp.asarray(e, BF16)


def _chunk_bits(any_sel, sub_per_tile, blocks_per_chunk):
    b, nq, _, n_slc = any_sel.shape
    g = NSA_GROUPS
    n_chunks = n_slc // blocks_per_chunk
    a = any_sel.reshape(b, nq, sub_per_tile, g, n_chunks, blocks_per_chunk).max(axis=-1) > 0.5
    a = a.transpose(0, 3, 1, 2, 4).reshape(b * g * nq * sub_per_tile, n_chunks)
    weights = jnp.left_shift(jnp.uint32(1), jnp.arange(n_chunks, dtype=jnp.uint32))
    return lax.bitcast_convert_type(jnp.sum(jnp.where(a, weights[None, :], jnp.uint32(0)), axis=1,
                                            dtype=jnp.uint32), jnp.int32)


def _diff_mixer(x, mod, layer, w_in, w_out, lam_q1, lam_k1, lam_q2, lam_k2, subln_g):
    n_qk = 2 * DIFF_HEADS * 2 * DIFF_HEAD_DIM
    n_v = DIFF_HEADS * DIFF_V_DIM
    w = w_in.astype(BF16)
    qk, vt = _modulated_projection(x, mod, layer, w[:, :n_qk], [(0, n_qk, "id")], [BF16],
                                   w[:, n_qk:].T, [(0, n_v)])
    lam_vecs = jnp.stack([lam_q1, lam_k1, lam_q2, lam_k2]).astype(F32)
    o = _diff_attention(qk, vt, lam_vecs, subln_g.astype(F32), layer)
    return o, w_out.astype(BF16)


def _nsa_mixer(x, mod, layer, w_in, pos_k, pos_v, k_w1, k_w2, v_w1, v_w2, w_out):
    b, t, _ = x.shape
    g, d = NSA_GROUPS, NSA_HEAD_DIM
    hd = NSA_HEADS * d
    cols = _nsa_weight_layout()
    w = _permute_columns(w_in, cols).astype(BF16)
    n_main = hd + 2 * (2 * g * d)
    n_cmp_cols = 2 * g * d
    plan = [(0, n_main, "id"), (n_main, n_main + n_cmp_cols, "id"),
            (n_main + n_cmp_cols, n_main + n_cmp_cols + g * LANES, "sigmoid")]
    v_cols = np.concatenate([np.arange(hd + 6 * g * d)[hd + (2 * n + 1) * g * d: hd + (2 * n + 2) * g * d]
                             for n in (1, 2)])
    wt = jnp.take(w_in, jnp.asarray(v_cols), axis=1).T.astype(BF16)
    qkv, kvc_tok, gates, vst, vwt = _modulated_projection(
        x, mod, layer, w, plan, [BF16, F32, F32], wt, [(0, g * d), (g * d, 2 * g * d)])

    rows = t // NSA_CMP_STRIDE
    pos = jnp.stack([pos_k.reshape(1, -1), pos_v.reshape(1, -1)]).astype(F32)
    w1 = jnp.stack([k_w1, v_w1]).astype(BF16)
    zeros = jnp.zeros((NSA_CMP_HIDDEN, d), F32)
    w2pad = jnp.stack([jnp.concatenate([k_w2, zeros], axis=1),
                       jnp.concatenate([zeros, v_w2], axis=1)]).astype(BF16)
    kvc, vct = _compress(kvc_tok, pos, w1, w2pad)

    n_slc = t // NSA_SLC_LEN
    tq_sel = min(SEL_Q_TILE, t)
    tq_cmp = min(CMP_Q_TILE, t)
    tk = min(SEL_K_TILE, t)
    oc, selt, any_sel = _nsa_cmp_topk(qkv, kvc, vct, gates, _overlap_t(rows, n_slc), tq_cmp, tq_sel)
    bits = _chunk_bits(any_sel, tq_cmp // tq_sel, tk // NSA_SLC_LEN)
    o = _nsa_sel_win(bits, qkv, vst, vwt, selt, oc, gates, _block_expander(n_slc, tk), tq_sel, tk)
    return o, w_out.astype(BF16)


def kernel(x, c, diff_w_in, diff_w_out, diff_lam_q1, diff_lam_k1, diff_lam_q2, diff_lam_k2, diff_subln_g,
           nsa_w_in, nsa_cmp_pos_k, nsa_cmp_pos_v, nsa_cmp_k_w1, nsa_cmp_k_w2, nsa_cmp_v_w1, nsa_cmp_v_w2,
           nsa_w_out, ada_w, ada_b, ln_mix_g, ln_mix_b, ln_ffn_g, ln_ffn_b, ffn_w_gu, ffn_w_down):
    mod = _ada_modulation(c, ada_w, ada_b)
    for i in range(DEPTH):
        j = i // N_MIXERS
        if i % N_MIXERS == 0:
            o, w_out = _diff_mixer(x, mod, i, diff_w_in[j], diff_w_out[j], diff_lam_q1[j], diff_lam_k1[j],
                                   diff_lam_q2[j], diff_lam_k2[j], diff_subln_g[j])
        else:
            o, w_out = _nsa_mixer(x, mod, i, nsa_w_in[j], nsa_cmp_pos_k[j], nsa_cmp_pos_v[j],
                                  nsa_cmp_k_w1[j], nsa_cmp_k_w2[j], nsa_cmp_v_w1[j], nsa_cmp_v_w2[j],
                                  nsa_w_out[j])
        x = _mix_ffn(o, w_out, x, mod, i, ln_mix_g[i], ln_mix_b[i],
                     ffn_w_gu[i].astype(BF16), ffn_w_down[i].astype(BF16), ln_ffn_g[i], ln_ffn_b[i])
    return x
```

```python
import functools
import math

import numpy as np
import jax
import jax.numpy as jnp
from jax import lax
from jax.experimental import pallas as pl
from jax.experimental.pallas import tpu as pltpu

F32 = jnp.float32
BF16 = jnp.bfloat16

D_MODEL = 1024
DEPTH = 2
N_MIXERS = 2
DEEPNORM_ALPHA = (2 * DEPTH) ** 0.25
LN_EPS = 1e-5
RMS_EPS = 1e-5
NEG_INF = -1e30
FORCED_SCORE = 1e6

DIFF_HEADS = 8
DIFF_HEAD_DIM = 64
DIFF_V_DIM = 2 * DIFF_HEAD_DIM

NSA_HEADS = 16
NSA_GROUPS = 4
NSA_HEAD_DIM = 64
NSA_HPG = NSA_HEADS // NSA_GROUPS
NSA_CMP_LEN = 32
NSA_CMP_STRIDE = 16
NSA_CMP_HIDDEN = 128
NSA_SLC_LEN = 64
NSA_SLC_TOPK = 16
NSA_WINDOW = 512

FFN_HIDDEN = -(-8 * D_MODEL // (3 * 256)) * 256

LANES = 128
VMEM_LIMIT = 56 * 1024 * 1024
PROJ_ROWS = 512
FFN_ROW_PARTS = 2
DIFF_TILE = 256
DIFF_STREAMS = 2
DIFF_ELIDE_LOG = 110.0
DIFF_ELIDE_SLACK = 1.02
CMP_Q_TILE = 256
SEL_Q_TILE = 128
SEL_K_TILE = 256
SEL_STREAMS = 2
M_FLOOR = -1e29
TAKEN = -3e38
ONES_ROWS = 16


def _cparams(sem):
    return pltpu.CompilerParams(dimension_semantics=sem, vmem_limit_bytes=VMEM_LIMIT)


def _dot(a, b):
    return jnp.dot(a, b, preferred_element_type=F32)


def _dot_nt(a, b):
    return lax.dot_general(a, b, (((1,), (1,)), ((), ())), preferred_element_type=F32)


def _layer_norm(z, g, b):
    mu = jnp.mean(z, axis=-1, keepdims=True)
    zc = z - mu
    var = jnp.mean(zc * zc, axis=-1, keepdims=True)
    return zc * lax.rsqrt(var + LN_EPS) * g + b


def _alibi_slopes(n_heads):
    return (2.0 ** (-8.0 * np.arange(1, n_heads + 1, dtype=np.float32) / n_heads)).astype(np.float32)


def _ada_kernel(c_ref, w_ref, b_ref, o_ref):
    c = c_ref[...]
    s = (c * jax.nn.sigmoid(c)).astype(BF16)
    o_ref[0] = _dot(s, w_ref[0].astype(BF16)) + b_ref[0]


def _ada_modulation(c, ada_w, ada_b):
    depth, d, d6 = ada_w.shape
    b = c.shape[0]
    rows = 8
    c_pad = jnp.zeros((rows, d), F32).at[:b].set(c)
    out = pl.pallas_call(
        _ada_kernel,
        out_shape=jax.ShapeDtypeStruct((depth, rows, d6), F32),
        grid=(depth, d6 // d),
        in_specs=[
            pl.BlockSpec((rows, d), lambda l, j: (0, 0)),
            pl.BlockSpec((1, d, d), lambda l, j: (l, 0, j)),
            pl.BlockSpec((1, 1, d), lambda l, j: (l, 0, j)),
        ],
        out_specs=pl.BlockSpec((1, rows, d), lambda l, j: (l, 0, j)),
        compiler_params=_cparams(("arbitrary", "arbitrary")),
        name="ada_modulation",
    )(c_pad, ada_w, ada_b.reshape(depth, 1, d6))
    return out[:, :b].reshape(depth, b, 6, 1, d)


def _mod_spec(layer, which, n_grid):
    d = D_MODEL
    if n_grid == 2:
        return pl.BlockSpec((None, None, None, 1, d), lambda b, i: (layer, b, which, 0, 0))
    return pl.BlockSpec((None, None, None, 1, d), lambda b, i, k: (layer, b, which, 0, 0))


def _proj_kernel(x_ref, sc_ref, sh_ref, w_ref, wt_ref, *out_refs, plan, t_plan):
    h = (x_ref[...] * (1.0 + sc_ref[...]) + sh_ref[...]).astype(BF16)
    for ref, (c0, c1, kind), in zip(out_refs, plan):
        step = 1024
        for s0 in range(c0, c1, step):
            s1 = min(s0 + step, c1)
            y = _dot(h, w_ref[:, s0:s1])
            if kind == "sigmoid":
                y = jax.nn.sigmoid(y)
            ref[:, s0 - c0:s1 - c0] = y.astype(ref.dtype)
    for ref, (r0, r1) in zip(out_refs[len(plan):], t_plan):
        ref[...] = _dot_nt(wt_ref[r0:r1, :], h).astype(ref.dtype)


def _modulated_projection(x, mod, layer, w, plan, out_dtypes, wt, t_plan):
    b, t, d = x.shape
    n = w.shape[1]
    tm = min(PROJ_ROWS, t)
    out_shape = [jax.ShapeDtypeStruct((b, t, c1 - c0), dt) for (c0, c1, _), dt in zip(plan, out_dtypes)]
    out_specs = [pl.BlockSpec((None, tm, c1 - c0), lambda bb, i: (bb, i, 0)) for (c0, c1, _) in plan]
    out_shape += [jax.ShapeDtypeStruct((b, r1 - r0, t), BF16) for (r0, r1) in t_plan]
    out_specs += [pl.BlockSpec((None, r1 - r0, tm), lambda bb, i: (bb, 0, i)) for (r0, r1) in t_plan]
    return pl.pallas_call(
        functools.partial(_proj_kernel, plan=plan, t_plan=t_plan),
        out_shape=out_shape,
        grid=(b, t // tm),
        in_specs=[
            pl.BlockSpec((None, tm, d), lambda bb, i: (bb, i, 0)),
            _mod_spec(layer, 1, 2),
            _mod_spec(layer, 0, 2),
            pl.BlockSpec((d, n), lambda bb, i: (0, 0)),
            pl.BlockSpec(wt.shape, lambda bb, i: (0, 0)),
        ],
        out_specs=out_specs,
        compiler_params=_cparams(("arbitrary", "arbitrary")),
        name=f"mod_proj_l{layer}",
    )(x, mod, mod, w, wt)


def _diff_attn_kernel(slope_ref, lam_ref, g_ref, q_ref, k_ref, vt_ref, o_ref, kn_ref, *scratch,
                      tile, lam_init, n_streams):
    h = pl.program_id(1)
    qi = pl.program_id(2)
    slope = slope_ref[0, h]
    inv_slope = slope_ref[1, h]
    half = DIFF_HEAD_DIM
    s_refs, mc_refs, st_refs, acc_refs = (scratch[n::4] for n in range(4))

    key = lax.broadcasted_iota(jnp.int32, (tile, tile), 0)
    qry = lax.broadcasted_iota(jnp.int32, (tile, tile), 1)
    lane = lax.broadcasted_iota(jnp.int32, (tile, LANES), 1)
    key_pos = lax.broadcasted_iota(jnp.int32, (tile, LANES), 0).astype(F32) * slope
    k_extra = jnp.where(lane == 0, 1.0, jnp.where(lane == 1, key_pos, 0.0)).astype(BF16)
    row = lax.broadcasted_iota(jnp.int32, (LANES, tile), 0)
    qry_pos = lax.broadcasted_iota(jnp.int32, (LANES, tile), 1).astype(F32) * slope
    q_extra = jnp.where(row == 0, -qry_pos, jnp.where(row == 1, 1.0, 0.0)).astype(BF16)
    ones_rows = jnp.ones((ONES_ROWS, tile), BF16)

    def max_component_norm(x, lane_ids):
        sq = x * x
        lo = jnp.sum(jnp.where(lane_ids < half, sq, 0.0), axis=1, keepdims=True)
        hi = jnp.sum(jnp.where(lane_ids >= half, sq, 0.0), axis=1, keepdims=True)
        return jnp.max(jnp.sqrt(jnp.max(jnp.maximum(lo, hi), axis=0, keepdims=True)))

    @pl.when(qi == 0)
    def _():
        for n in range(n_streams):
            kf = k_ref[n].astype(F32)
            kn_ref[n] = max_component_norm(kf, lax.broadcasted_iota(jnp.int32, kf.shape, 1))

    qcs = []
    reach_max = jnp.float32(0.0)
    for n in range(n_streams):
        qf = q_ref[n].astype(F32) * (DIFF_HEAD_DIM ** -0.5)
        qcs.append(tuple(jnp.concatenate([jnp.where(sel, qf, 0.0).T.astype(BF16), q_extra], axis=0)
                         for sel in (lane < half, lane >= half)))
        qn = max_component_norm(qf, lane)
        reach = (DIFF_ELIDE_LOG + DIFF_ELIDE_SLACK * 2.0 * qn * kn_ref[n]) * inv_slope
        reach_max = jnp.maximum(reach_max, reach)
        acc_refs[n][...] = jnp.zeros(acc_refs[n].shape, F32)
        st_refs[n][...] = jnp.full(st_refs[n].shape, M_FLOOR, F32)
    n_keep = lax.fori_loop(
        0, qi, lambda a, cnt: cnt + jnp.where((a * tile + 1).astype(F32) < reach_max, 1, 0), 0)
    j0 = qi - n_keep

    def scores(n, j):
        k = k_ref[n, pl.ds(pl.multiple_of(j * tile, tile), tile), :]
        k = jnp.concatenate([k, k_extra], axis=1)
        return [_dot(k, qcs[n][c]) for c in range(2)]

    def produce(n, j, raw, diagonal):
        slot = j & 1
        for c in range(2):
            s = raw[c]
            if diagonal is True:
                s = jnp.where(key <= qry, s, NEG_INF)
            elif diagonal is not False:
                s = jnp.where((key <= qry) | jnp.logical_not(diagonal), s, NEG_INF)
            s_refs[n][slot, c] = s
            mc_refs[n][slot, c:c + 1, :] = jnp.max(s, axis=0, keepdims=True)

    def consume(n, j):
        slot = j & 1
        vt = vt_ref[n, :, pl.ds(pl.multiple_of(j * tile, tile), tile)]
        vt = jnp.concatenate([vt, ones_rows], axis=0)
        off = slope * ((qi - j) * tile).astype(F32)
        st_ref, acc_ref = st_refs[n], acc_refs[n]
        for c in range(2):
            m_prev = st_ref[c:c + 1, :]
            m_next = jnp.maximum(m_prev, mc_refs[n][slot, c:c + 1, :] - off)
            alpha = jnp.exp(m_prev - m_next)
            p = jnp.exp(s_refs[n][slot, c] - (m_next + off))
            st_ref[c:c + 1, :] = m_next
            acc_ref[c] = alpha * acc_ref[c] + _dot(vt, p.astype(BF16))

    for n in range(n_streams):
        produce(n, j0, scores(n, j0), j0 == qi)

    def step(t, diagonal):
        raws = [scores(n, t) for n in range(n_streams)]
        for n in range(n_streams):
            consume(n, t - 1)
            produce(n, t, raws[n], diagonal)

    def body(t, carry):
        step(t, False)
        return carry

    lax.fori_loop(j0 + 1, qi, body, 0)

    @pl.when(qi > j0)
    def _():
        step(qi, True)

    lam_v = lam_ref[...]
    lam = (jnp.exp(jnp.sum(lam_v[0:1] * lam_v[1:2], axis=1, keepdims=True))
           - jnp.exp(jnp.sum(lam_v[2:3] * lam_v[3:4], axis=1, keepdims=True)) + lam_init)
    for n in range(n_streams):
        consume(n, qi)
        dv = DIFF_V_DIM
        l0 = acc_refs[n][0, dv:dv + 1, :]
        l1 = acc_refs[n][1, dv:dv + 1, :]
        o = acc_refs[n][0, :dv, :] / l0 - lam * (acc_refs[n][1, :dv, :] / l1)
        o = o * lax.rsqrt(jnp.mean(o * o, axis=0, keepdims=True) + RMS_EPS)
        o_ref[n] = ((o * g_ref[...] * (1.0 - lam_init)).T).astype(o_ref.dtype)


def _diff_attention(qk, vt, lam_vecs, subln_g, layer_idx):
    b, t, _ = qk.shape
    hh = DIFF_HEADS
    tile = min(DIFF_TILE, t)
    ns = DIFF_STREAMS if b % DIFF_STREAMS == 0 else 1
    lam_init = 0.8 - 0.6 * math.exp(-0.3 * layer_idx)
    slopes = _alibi_slopes(hh)
    slope_tab = jnp.asarray(np.stack([slopes, 1.0 / slopes]))
    per_stream = [
        pltpu.VMEM((2, 2, tile, tile), F32),
        pltpu.VMEM((2, 8, tile), F32),
        pltpu.VMEM((8, tile), F32),
        pltpu.VMEM((2, DIFF_V_DIM + ONES_ROWS, tile), F32),
    ]
    col = (slopes[:, None] * np.arange(tile, dtype=np.float32)[None, :]).astype(np.float32)
    assert np.array_equal(col.astype(jnp.bfloat16).astype(np.float32), col)
    return pl.pallas_call(
        functools.partial(_diff_attn_kernel, tile=tile, lam_init=lam_init, n_streams=ns),
        out_shape=jax.ShapeDtypeStruct((b, t, hh * DIFF_V_DIM), BF16),
        grid=(b // ns, hh, t // tile),
        in_specs=[
            pl.BlockSpec(memory_space=pltpu.SMEM),
            pl.BlockSpec((4, DIFF_HEAD_DIM), lambda bb, h, i: (0, 0)),
            pl.BlockSpec((DIFF_V_DIM, 1), lambda bb, h, i: (0, 0)),
            pl.BlockSpec((ns, tile, LANES), lambda bb, h, i: (bb, i, h)),
            pl.BlockSpec((ns, t, LANES), lambda bb, h, i: (bb, 0, hh + h)),
            pl.BlockSpec((ns, DIFF_V_DIM, t), lambda bb, h, i: (bb, h, 0)),
        ],
        out_specs=pl.BlockSpec((ns, tile, LANES), lambda bb, h, i: (bb, i, h)),
        scratch_shapes=[pltpu.SMEM((ns,), F32)] + per_stream * ns,
        compiler_params=_cparams(("arbitrary", "arbitrary", "arbitrary")),
        name="diff_attention",
    )(slope_tab, lam_vecs, subln_g.reshape(DIFF_V_DIM, 1), qk, qk, vt)


def _mix_ffn_kernel(o_ref, wo_ref, x_ref, ga_ref, lng_a_ref, lnb_a_ref, sc_ref, sh_ref, gf_ref,
                    wgu_ref, wd_ref, lng_f_ref, lnb_f_ref, out_ref, *, n_parts):
    rows = o_ref.shape[0] // n_parts
    fh = wd_ref.shape[0]
    for r in range(n_parts):
        rs = slice(r * rows, (r + 1) * rows)
        y = _dot(o_ref[rs, :], wo_ref[...])
        z = DEEPNORM_ALPHA * x_ref[rs, :] + (1.0 + ga_ref[...]) * y
        x1 = _layer_norm(z, lng_a_ref[...], lnb_a_ref[...])
        h = (x1 * (1.0 + sc_ref[...]) + sh_ref[...]).astype(BF16)
        gp = _dot(h, wgu_ref[:, :fh])
        up = _dot(h, wgu_ref[:, fh:])
        act = (gp * jax.nn.sigmoid(gp) * up).astype(BF16)
        z = DEEPNORM_ALPHA * x1 + (1.0 + gf_ref[...]) * _dot(act, wd_ref[...])
        out_ref[rs, :] = _layer_norm(z, lng_f_ref[...], lnb_f_ref[...])


def _mix_ffn(o, w_out, x, mod, layer, ln_mix_g, ln_mix_b, w_gu, w_down, ln_ffn_g, ln_ffn_b):
    b, t, d = x.shape
    dh = o.shape[-1]
    tm = min(PROJ_ROWS, t)
    once = pl.Buffered(1)
    row_vec = pl.BlockSpec((1, d), lambda bb, i: (0, 0))
    return pl.pallas_call(
        functools.partial(_mix_ffn_kernel, n_parts=FFN_ROW_PARTS),
        out_shape=jax.ShapeDtypeStruct((b, t, d), F32),
        grid=(b, t // tm),
        in_specs=[
            pl.BlockSpec((None, tm, dh), lambda bb, i: (bb, i, 0)),
            pl.BlockSpec((dh, d), lambda bb, i: (0, 0), pipeline_mode=once),
            pl.BlockSpec((None, tm, d), lambda bb, i: (bb, i, 0)),
            _mod_spec(layer, 2, 2),
            row_vec,
            row_vec,
            _mod_spec(layer, 4, 2),
            _mod_spec(layer, 3, 2),
            _mod_spec(layer, 5, 2),
            pl.BlockSpec(w_gu.shape, lambda bb, i: (0, 0), pipeline_mode=once),
            pl.BlockSpec(w_down.shape, lambda bb, i: (0, 0), pipeline_mode=once),
            row_vec,
            row_vec,
        ],
        out_specs=pl.BlockSpec((None, tm, d), lambda bb, i: (bb, i, 0)),
        compiler_params=_cparams(("arbitrary", "arbitrary")),
        name=f"mix_ffn_l{layer}",
    )(o, w_out, x, mod, ln_mix_g.reshape(1, d), ln_mix_b.reshape(1, d), mod, mod, mod,
      w_gu, w_down, ln_ffn_g.reshape(1, d), ln_ffn_b.reshape(1, d))


def _gelu_tanh(x):
    return 0.5 * x * (1.0 + jnp.tanh(math.sqrt(2.0 / math.pi) * (x + 0.044715 * (x * x * x))))


def _compress_kernel(x_ref, pos_ref, w1_ref, w2_ref, o_ref, vt_ref):
    d = NSA_HEAD_DIM
    n_rows, stride, _ = x_ref.shape
    half = stride * d
    lane = lax.broadcasted_iota(jnp.int32, (n_rows, LANES), 1)
    pieces = ([], [])
    for lp in range(stride // 2):
        even = x_ref[:, 2 * lp, :]
        odd = x_ref[:, 2 * lp + 1, :]
        pieces[0].append(jnp.where(lane < d, even, pltpu.roll(odd, d, 1)))
        pieces[1].append(jnp.where(lane < d, pltpu.roll(even, d, 1), odd))
    out = None
    for kv in range(2):
        r = jnp.concatenate(pieces[kv], axis=1)
        r_next = pltpu.roll(r, n_rows - 1, 0)
        a = (r + pos_ref[kv, :, :half]).astype(BF16)
        bnx = (r_next + pos_ref[kv, :, half:]).astype(BF16)
        hid = _dot(a, w1_ref[kv, :half, :]) + _dot(bnx, w1_ref[kv, half:, :])
        y = _dot(_gelu_tanh(hid).astype(BF16), w2_ref[kv])
        out = y if out is None else out + y
    o_ref[...] = out.astype(o_ref.dtype)
    vt_ref[...] = out.T[NSA_HEAD_DIM:, :].astype(vt_ref.dtype)


def _compress(kvc_tok, pos, w1, w2pad):
    b, t, _ = kvc_tok.shape
    g = NSA_GROUPS
    rows = t // NSA_CMP_STRIDE
    width = NSA_CMP_STRIDE * NSA_HEAD_DIM
    return pl.pallas_call(
        _compress_kernel,
        out_shape=[jax.ShapeDtypeStruct((b, g, rows, LANES), BF16),
                   jax.ShapeDtypeStruct((b, g, NSA_HEAD_DIM, rows), BF16)],
        grid=(b, g),
        in_specs=[
            pl.BlockSpec((None, rows, NSA_CMP_STRIDE, LANES), lambda bb, gg: (bb, 0, 0, gg)),
            pl.BlockSpec((2, 1, 2 * width), lambda bb, gg: (0, 0, 0)),
            pl.BlockSpec((2, 2 * width, NSA_CMP_HIDDEN), lambda bb, gg: (0, 0, 0)),
            pl.BlockSpec((2, NSA_CMP_HIDDEN, LANES), lambda bb, gg: (0, 0, 0)),
        ],
        out_specs=[pl.BlockSpec((None, None, rows, LANES), lambda bb, gg: (bb, gg, 0, 0)),
                   pl.BlockSpec((None, None, NSA_HEAD_DIM, rows), lambda bb, gg: (bb, gg, 0, 0))],
        compiler_params=_cparams(("arbitrary", "arbitrary")),
        name="nsa_compress",
    )(kvc_tok.reshape(b, rows, NSA_CMP_STRIDE, g * LANES), pos, w1, w2pad)


def _head_query_t(q_pair, head_in_pair, lane):
    d = NSA_HEAD_DIM
    qf = q_pair.astype(F32) * (d ** -0.5)
    if head_in_pair == 1:
        qf = pltpu.roll(qf, d, 1)
    return jnp.where(lane < d, qf, 0.0).T.astype(BF16)


def _merge_head_pair(o_even, o_odd, lane):
    return jnp.where(lane < NSA_HEAD_DIM, pltpu.roll(o_even, NSA_HEAD_DIM, 1), o_odd)


def _nsa_cmp_topk_kernel(slope_ref, q_ref, kvc_ref, vct_ref, gates_ref, ovt_ref,
                         oc_ref, sel_ref, any_ref, sdc_ref, *, tq, n_rows, n_slc, k_sel, sub_tile):
    i = pl.program_id(1)
    q0 = i * tq
    hpg = NSA_HPG
    d = NSA_HEAD_DIM
    lane = lax.broadcasted_iota(jnp.int32, (tq, LANES), 1)
    qry_c = lax.broadcasted_iota(jnp.int32, (n_rows, tq), 1)
    end_c = lax.broadcasted_iota(jnp.int32, (n_rows, tq), 0) * NSA_CMP_STRIDE + (NSA_CMP_LEN - 1)

    @pl.when(i == 0)
    def _():
        rel_c = (qry_c - end_c).astype(F32)
        for h in range(NSA_HEADS):
            sdc_ref[:, h * tq:(h + 1) * tq] = rel_c * slope_ref[h]

    valid = end_c <= q0 + qry_c

    blk_i = lax.broadcasted_iota(jnp.int32, (n_slc, tq), 0)
    qpos = q0 + lax.broadcasted_iota(jnp.int32, (n_slc, tq), 1)
    q_blk = lax.shift_right_logical(qpos, int(math.log2(NSA_SLC_LEN)))
    forced = (blk_i == 0) | (blk_i == q_blk) | (blk_i == q_blk - 1)
    causal_blk = blk_i * NSA_SLC_LEN <= qpos
    blk_f = blk_i.astype(F32)

    for g in range(NSA_GROUPS):
        q4t = jnp.concatenate(
            [_head_query_t(q_ref[:, ((g * hpg + hh) // 2) * LANES:((g * hpg + hh) // 2 + 1) * LANES], hh % 2, lane)
             for hh in range(hpg)], axis=1)
        raw = _dot(kvc_ref[g], q4t)
        s = jnp.concatenate(
            [jnp.where(valid, raw[:, hh * tq:(hh + 1) * tq]
                       - sdc_ref[:, (g * hpg + hh) * tq:(g * hpg + hh + 1) * tq], NEG_INF)
             for hh in range(hpg)], axis=1)
        m = jnp.max(s, axis=0, keepdims=True)
        p = jnp.exp(s - m)
        l = jnp.sum(p, axis=0, keepdims=True)
        inv = jnp.where(m > 0.5 * NEG_INF, 1.0 / l, 0.0)
        lhs = jnp.concatenate([vct_ref[g], ovt_ref[...]], axis=0)
        r = _dot(lhs, p.astype(BF16))
        gates_t = gates_ref[:, g * LANES:(g + 1) * LANES].T
        w = inv * jnp.concatenate([gates_t[3 * hh:3 * hh + 1] for hh in range(hpg)], axis=1)
        oc_t = r[:d] * w
        by_head = jnp.concatenate([oc_t[:, hh * tq:(hh + 1) * tq] for hh in range(hpg)], axis=0)
        oc_ref[:, g * hpg * d:(g + 1) * hpg * d] = by_head.T
        imp_t = r[d:, 0:tq] * inv[:, 0:tq]
        for hh in range(1, hpg):
            imp_t = imp_t + r[d:, hh * tq:(hh + 1) * tq] * inv[:, hh * tq:(hh + 1) * tq]

        imp = jnp.where(causal_blk, jnp.where(forced, FORCED_SCORE, imp_t), NEG_INF)
        for _ in range(k_sel):
            mx = jnp.max(imp, axis=0, keepdims=True)
            cand = jnp.where(imp == mx, blk_f, float(n_slc))
            pick = blk_f == jnp.min(cand, axis=0, keepdims=True)
            imp = jnp.where(pick, TAKEN, imp)
        sel_b = jnp.where(causal_blk & (imp < 0.5 * TAKEN), 1.0, 0.0).astype(BF16)
        sel_ref[g] = sel_b
        ones = jnp.ones((8, sub_tile), BF16)
        for sub in range(tq // sub_tile):
            r = sub * NSA_GROUPS + g
            cnt = _dot_nt(ones, sel_b[:, sub * sub_tile:(sub + 1) * sub_tile])
            any_ref[r:r + 1, :] = cnt[0:1]


def _nsa_cmp_topk(qkv, kvc, vct, gates, ovt, tq, sub_tile):
    b, t, _ = qkv.shape
    g = NSA_GROUPS
    n_rows = kvc.shape[2]
    n_slc = t // NSA_SLC_LEN
    k_sel = min(NSA_SLC_TOPK, n_slc)
    nq = t // tq
    assert n_slc == LANES and (tq // sub_tile) * g == 8
    slopes = jnp.asarray(_alibi_slopes(NSA_HEADS))
    return pl.pallas_call(
        functools.partial(_nsa_cmp_topk_kernel, tq=tq, n_rows=n_rows, n_slc=n_slc, k_sel=k_sel,
                          sub_tile=sub_tile),
        out_shape=[
            jax.ShapeDtypeStruct((b, t, NSA_HEADS * NSA_HEAD_DIM), F32),
            jax.ShapeDtypeStruct((b, g, n_slc, t), BF16),
            jax.ShapeDtypeStruct((b, nq, 8, n_slc), F32),
        ],
        grid=(b, nq),
        in_specs=[
            pl.BlockSpec(memory_space=pltpu.SMEM),
            pl.BlockSpec((None, tq, NSA_HEADS * NSA_HEAD_DIM), lambda bb, i: (bb, i, 0)),
            pl.BlockSpec((None, g, n_rows, LANES), lambda bb, i: (bb, 0, 0, 0)),
            pl.BlockSpec((None, g, NSA_HEAD_DIM, n_rows), lambda bb, i: (bb, 0, 0, 0)),
            pl.BlockSpec((None, tq, g * LANES), lambda bb, i: (bb, i, 0)),
            pl.BlockSpec((n_slc, n_rows), lambda bb, i: (0, 0)),
        ],
        out_specs=[
            pl.BlockSpec((None, tq, NSA_HEADS * NSA_HEAD_DIM), lambda bb, i: (bb, i, 0)),
            pl.BlockSpec((None, g, n_slc, tq), lambda bb, i: (bb, 0, 0, i)),
            pl.BlockSpec((None, None, 8, n_slc), lambda bb, i: (bb, i, 0, 0)),
        ],
        scratch_shapes=[pltpu.VMEM((n_rows, NSA_HEADS * tq), F32)],
        compiler_params=_cparams(("arbitrary", "arbitrary")),
        name="nsa_cmp_topk",
    )(slopes, qkv, kvc, vct, gates, ovt)


def _nsa_sel_win_kernel(bits_ref, slope_ref, q_ref, kvs_ref, vst_ref, kvw_ref, vwt_ref, selt_ref, oc_ref,
                        gates_ref, eselt_ref, o_ref, list_ref, qx_ref, *scratch,
                        tq, tk, win_span, n_chunks, n_streams):
    b = pl.program_id(0)
    g = pl.program_id(1)
    i = pl.program_id(2)
    hpg = NSA_HPG
    d = NSA_HEAD_DIM
    wide = hpg * tq
    wb_refs, s_refs, mc_refs, st_refs, acc_refs = (scratch[n::5] for n in range(5))
    lane = lax.broadcasted_iota(jnp.int32, (tq, LANES), 1)
    slopes = [slope_ref[g * hpg + hh] for hh in range(hpg)]
    q0s = [(i * n_streams + n) * tq for n in range(n_streams)]

    def per_head_row(vals):
        return jnp.concatenate([jnp.full((1, tq), v, F32) for v in vals], axis=1)

    q4s = [jnp.concatenate(
        [_head_query_t(q_ref[n * tq:(n + 1) * tq, (hh // 2) * LANES:(hh // 2 + 1) * LANES], hh % 2, lane)
         for hh in range(hpg)], axis=1) for n in range(n_streams)]

    rel = (lax.broadcasted_iota(jnp.int32, (tk, tq), 1)
           - lax.broadcasted_iota(jnp.int32, (tk, tq), 0))

    def split3(x):
        hi = x.astype(BF16)
        r1 = x - hi.astype(F32)
        mid = r1.astype(BF16)
        lo = (r1 - mid.astype(F32)).astype(BF16)
        return [hi.astype(F32), mid.astype(F32), lo.astype(F32)]

    @pl.when(i == 0)
    def _():
        qry = lax.broadcasted_iota(jnp.int32, (1, tq), 1).astype(F32)
        pieces = (split3(jnp.concatenate([-slopes[hh] * qry for hh in range(hpg)], axis=1))
                  + split3(per_head_row(slopes)))
        row = lax.broadcasted_iota(jnp.int32, (LANES, wide), 0)
        qx = jnp.zeros((LANES, wide), F32)
        for r, piece in enumerate(pieces):
            qx = jnp.where(row == r, piece, qx)
        qx_ref[...] = qx.astype(BF16)

    key_f = lax.broadcasted_iota(jnp.int32, (tk, LANES), 0).astype(F32)
    lane_k = lax.broadcasted_iota(jnp.int32, (tk, LANES), 1)
    k_extra = jnp.where(lane_k < 3, 1.0, jnp.where(lane_k < 6, key_f, 0.0)).astype(BF16)
    q4a = [jnp.concatenate([q4s[n], qx_ref[...]], axis=0) for n in range(n_streams)]

    bits = bits_ref[(b * pl.num_programs(1) + g) * pl.num_programs(2) + i]

    def collect(j, n):
        hit = lax.shift_right_logical(bits, j) & 1

        @pl.when(hit == 1)
        def _():
            list_ref[n] = j
        return n + hit

    n_causal = (q0s[-1] + tq + tk - 1) // tk
    n_act = lax.fori_loop(0, n_causal, collect, 0)

    for n in range(n_streams):
        st_refs[n][0:1, :] = jnp.full((1, wide), M_FLOOR, F32)
        st_refs[n][1:2, :] = jnp.zeros((1, wide), F32)
        acc_refs[n][...] = jnp.zeros(acc_refs[n].shape, F32)
    selts = [selt_ref[:, n * tq:(n + 1) * tq] for n in range(n_streams)]

    def scores(n, j):
        kv = kvs_ref[pl.ds(pl.multiple_of(j * tk, tk), tk), :]
        kv = jnp.concatenate([kv, k_extra], axis=1)
        return _dot(kv, q4a[n])

    def produce(n, j, raw, slot):
        picked = _dot(eselt_ref[j], selts[n])
        causal = rel + (q0s[n] - j * tk) >= 0
        madd = jnp.where((picked > 0.5) & causal, 0.0, NEG_INF)
        s = raw + jnp.concatenate([madd] * hpg, axis=1)
        s_refs[n][slot] = s
        mc_refs[n][slot, 0:1, :] = jnp.max(s, axis=0, keepdims=True)

    def consume(n, j, slot):
        off = per_head_row([slopes[hh] * (q0s[n] - j * tk).astype(F32) for hh in range(hpg)])
        st_ref, acc_ref = st_refs[n], acc_refs[n]
        m_prev = st_ref[0:1, :]
        l_prev = st_ref[1:2, :]
        m_next = jnp.maximum(m_prev, mc_refs[n][slot, 0:1, :] - off)
        alpha = jnp.exp(m_prev - m_next)
        p = jnp.exp(s_refs[n][slot] - (m_next + off))
        st_ref[0:1, :] = m_next
        st_ref[1:2, :] = alpha * l_prev + jnp.sum(p, axis=0, keepdims=True)
        vt = vst_ref[:, pl.ds(pl.multiple_of(j * tk, tk), tk)]
        acc_ref[...] = alpha * acc_ref[...] + _dot(vt, p.astype(BF16))

    j_first = list_ref[0]
    for n in range(n_streams):
        produce(n, j_first, scores(n, j_first), 0)

    def body(t, carry):
        j_new = list_ref[t]
        j_old = list_ref[t - 1]
        raws = [scores(n, j_new) for n in range(n_streams)]
        for n in range(n_streams):
            consume(n, j_old, (t - 1) & 1)
            produce(n, j_new, raws[n], t & 1)
        return carry

    lax.fori_loop(1, n_act, body, 0)
    j_last = list_ref[n_act - 1]

    for n in range(n_streams):
        q0 = q0s[n]
        consume(n, j_last, (n_act - 1) & 1)

        start = pl.multiple_of(jnp.maximum(q0 + tq - win_span, 0), tq)
        s_w = _dot(kvw_ref[pl.ds(start, win_span), :], q4s[n])

        @pl.when(q0 + tq - win_span < n_streams * tq)
        def _():
            dist_w = (lax.broadcasted_iota(jnp.int32, (win_span, tq), 1)
                      - lax.broadcasted_iota(jnp.int32, (win_span, tq), 0) + (q0 - start))
            valid_w = (dist_w >= 0) & (dist_w < NSA_WINDOW)
            dist_wf = dist_w.astype(F32)
            wb_refs[n][...] = jnp.concatenate(
                [jnp.where(valid_w, -slopes[hh] * dist_wf, NEG_INF) for hh in range(hpg)], axis=1)

        s_w = s_w + wb_refs[n][...]
        p_w = jnp.exp(s_w - jnp.max(s_w, axis=0, keepdims=True))
        vw = jnp.concatenate([vwt_ref[:, pl.ds(start, win_span)], jnp.ones((ONES_ROWS, win_span), BF16)], axis=0)
        r_w = _dot(vw, p_w.astype(BF16))
        o_w = r_w[:d] * (1.0 / r_w[d:d + 1])

        l_s = st_refs[n][1:2, :]
        o_s = acc_refs[n][...] * jnp.where(l_s > 0.0, 1.0 / l_s, 0.0)
        gates_t = gates_ref[n * tq:(n + 1) * tq, :].T
        g_s = jnp.concatenate([gates_t[3 * hh + 1:3 * hh + 2] for hh in range(hpg)], axis=1)
        g_w = jnp.concatenate([gates_t[3 * hh + 2:3 * hh + 3] for hh in range(hpg)], axis=1)
        comb = g_s * o_s + g_w * o_w
        by_head = jnp.concatenate([comb[:, hh * tq:(hh + 1) * tq] for hh in range(hpg)], axis=0)
        o_ref[n * tq:(n + 1) * tq, :] = (by_head.T + oc_ref[n * tq:(n + 1) * tq, :]).astype(o_ref.dtype)


def _nsa_sel_win(bits, qkv, vst, vwt, selt, oc, gates, eselt, tq, tk):
    b, t, _ = qkv.shape
    g = NSA_GROUPS
    d = NSA_HEAD_DIM
    n_slc = t // NSA_SLC_LEN
    nq = t // tq
    hd = NSA_HPG * d
    wide = NSA_HPG * tq
    kv_col0 = NSA_HEADS * d // LANES
    win_span = min(NSA_WINDOW + tq, t)
    n_chunks = t // tk
    ns = SEL_STREAMS if nq % SEL_STREAMS == 0 else 1
    rows = ns * tq
    assert tk <= 256
    tile_bits = bits.reshape(-1, ns)
    bits = functools.reduce(jnp.bitwise_or, [tile_bits[:, n] for n in range(ns)])
    slopes = jnp.asarray(_alibi_slopes(NSA_HEADS))
    per_stream = [
        pltpu.VMEM((win_span, wide), F32),
        pltpu.VMEM((2, tk, wide), F32),
        pltpu.VMEM((2, 8, wide), F32),
        pltpu.VMEM((8, wide), F32),
        pltpu.VMEM((d, wide), F32),
    ]
    grid_spec = pltpu.PrefetchScalarGridSpec(
        num_scalar_prefetch=1,
        grid=(b, g, nq // ns),
        in_specs=[
            pl.BlockSpec(memory_space=pltpu.SMEM),
            pl.BlockSpec((None, rows, hd), lambda bb, gg, i, bits: (bb, i, gg)),
            pl.BlockSpec((None, t, LANES), lambda bb, gg, i, bits: (bb, 0, kv_col0 + gg)),
            pl.BlockSpec((None, d, t), lambda bb, gg, i, bits: (bb, gg, 0)),
            pl.BlockSpec((None, t, LANES), lambda bb, gg, i, bits: (bb, 0, kv_col0 + g + gg)),
            pl.BlockSpec((None, d, t), lambda bb, gg, i, bits: (bb, gg, 0)),
            pl.BlockSpec((None, None, n_slc, rows), lambda bb, gg, i, bits: (bb, gg, 0, i)),
            pl.BlockSpec((None, rows, hd), lambda bb, gg, i, bits: (bb, i, gg)),
            pl.BlockSpec((None, rows, LANES), lambda bb, gg, i, bits: (bb, i, gg)),
            pl.BlockSpec(eselt.shape, lambda bb, gg, i, bits: (0, 0, 0)),
        ],
        out_specs=pl.BlockSpec((None, rows, hd), lambda bb, gg, i, bits: (bb, i, gg)),
        scratch_shapes=[pltpu.SMEM((n_chunks,), jnp.int32), pltpu.VMEM((LANES, wide), BF16)] + per_stream * ns,
    )
    return pl.pallas_call(
        functools.partial(_nsa_sel_win_kernel, tq=tq, tk=tk, win_span=win_span, n_chunks=n_chunks,
                          n_streams=ns),
        out_shape=jax.ShapeDtypeStruct((b, t, NSA_HEADS * d), BF16),
        grid_spec=grid_spec,
        compiler_params=_cparams(("arbitrary", "arbitrary", "arbitrary")),
        name="nsa_sel_win",
    )(bits, slopes, qkv, qkv, vst, qkv, vwt, selt, oc, gates, eselt)


def _nsa_weight_layout():
    hd, g, d = NSA_HEADS * NSA_HEAD_DIM, NSA_GROUPS, NSA_HEAD_DIM
    base = {name: hd + n * g * d for n, name in enumerate(["kc", "vc", "ks", "vs", "kw", "vw"])}
    gate0 = hd + 6 * g * d
    cols = list(range(hd))
    for kn, vn in (("ks", "vs"), ("kw", "vw"), ("kc", "vc")):
        for gg in range(g):
            cols += list(range(base[kn] + gg * d, base[kn] + (gg + 1) * d))
            cols += list(range(base[vn] + gg * d, base[vn] + (gg + 1) * d))
    for gg in range(g):
        n_gate = NSA_HPG * 3
        cols += list(range(gate0 + gg * n_gate, gate0 + (gg + 1) * n_gate)) + [-1] * (LANES - n_gate)
    return np.asarray(cols, np.int32)


def _permute_columns(w, cols):
    picked = jnp.take(w, jnp.asarray(np.maximum(cols, 0)), axis=1)
    return jnp.where(jnp.asarray(cols >= 0)[None, :], picked, 0.0)


def _overlap_t(n_rows, n_slc):
    n_cmp = n_rows - 1
    c_start = np.arange(n_rows) * NSA_CMP_STRIDE
    s_start = np.arange(n_slc) * NSA_SLC_LEN
    ov = (np.minimum(c_start[:, None] + NSA_CMP_LEN, s_start[None, :] + NSA_SLC_LEN)
          - np.maximum(c_start[:, None], s_start[None, :]))
    ov = np.clip(ov, 0, None).astype(np.float32) / NSA_CMP_LEN
    ov[n_cmp:] = 0.0
    return jnp.asarray(ov.T, BF16)


def _block_expander(n_slc, tk):
    n_chunks = n_slc * NSA_SLC_LEN // tk
    key_blk = (np.arange(n_chunks)[:, None] * tk + np.arange(tk)[None, :]) // NSA_SLC_LEN
    e = (key_blk[:, :, None] == np.arange(n_slc)[None, None, :]).astype(np.float32)
    return jnp.asarray(e, BF16)


def _chunk_bits(any_sel, sub_per_tile, blocks_per_chunk):
    b, nq, _, n_slc = any_sel.shape
    g = NSA_GROUPS
    n_chunks = n_slc // blocks_per_chunk
    a = any_sel.reshape(b, nq, sub_per_tile, g, n_chunks, blocks_per_chunk).max(axis=-1) > 0.5
    a = a.transpose(0, 3, 1, 2, 4).reshape(b * g * nq * sub_per_tile, n_chunks)
    weights = jnp.left_shift(jnp.uint32(1), jnp.arange(n_chunks, dtype=jnp.uint32))
    return lax.bitcast_convert_type(jnp.sum(jnp.where(a, weights[None, :], jnp.uint32(0)), axis=1,
                                            dtype=jnp.uint32), jnp.int32)


def _diff_mixer(x, mod, layer, w_in, w_out, lam_q1, lam_k1, lam_q2, lam_k2, subln_g):
    n_qk = 2 * DIFF_HEADS * 2 * DIFF_HEAD_DIM
    n_v = DIFF_HEADS * DIFF_V_DIM
    w = w_in.astype(BF16)
    qk, vt = _modulated_projection(x, mod, layer, w[:, :n_qk], [(0, n_qk, "id")], [BF16],
                                   w[:, n_qk:].T, [(0, n_v)])
    lam_vecs = jnp.stack([lam_q1, lam_k1, lam_q2, lam_k2]).astype(F32)
    o = _diff_attention(qk, vt, lam_vecs, subln_g.astype(F32), layer)
    return o, w_out.astype(BF16)


def _nsa_mixer(x, mod, layer, w_in, pos_k, pos_v, k_w1, k_w2, v_w1, v_w2, w_out):
    b, t, _ = x.shape
    g, d = NSA_GROUPS, NSA_HEAD_DIM
    hd = NSA_HEADS * d
    cols = _nsa_weight_layout()
    w = _permute_columns(w_in, cols).astype(BF16)
    n_main = hd + 2 * (2 * g * d)
    n_cmp_cols = 2 * g * d
    plan = [(0, n_main, "id"), (n_main, n_main + n_cmp_cols, "id"),
            (n_main + n_cmp_cols, n_main + n_cmp_cols + g * LANES, "sigmoid")]
    v_cols = np.concatenate([np.arange(hd + 6 * g * d)[hd + (2 * n + 1) * g * d: hd + (2 * n + 2) * g * d]
                             for n in (1, 2)])
    wt = jnp.take(w_in, jnp.asarray(v_cols), axis=1).T.astype(BF16)
    qkv, kvc_tok, gates, vst, vwt = _modulated_projection(
        x, mod, layer, w, plan, [BF16, F32, F32], wt, [(0, g * d), (g * d, 2 * g * d)])

    rows = t // NSA_CMP_STRIDE
    pos = jnp.stack([pos_k.reshape(1, -1), pos_v.reshape(1, -1)]).astype(F32)
    w1 = jnp.stack([k_w1, v_w1]).astype(BF16)
    zeros = jnp.zeros((NSA_CMP_HIDDEN, d), F32)
    w2pad = jnp.stack([jnp.concatenate([k_w2, zeros], axis=1),
                       jnp.concatenate([zeros, v_w2], axis=1)]).astype(BF16)
    kvc, vct = _compress(kvc_tok, pos, w1, w2pad)

    n_slc = t // NSA_SLC_LEN
    tq_sel = min(SEL_Q_TILE, t)
    tq_cmp = min(CMP_Q_TILE, t)
    tk = min(SEL_K_TILE, t)
    oc, selt, any_sel = _nsa_cmp_topk(qkv, kvc, vct, gates, _overlap_t(rows, n_slc), tq_cmp, tq_sel)
    bits = _chunk_bits(any_sel, tq_cmp // tq_sel, tk // NSA_SLC_LEN)
    o = _nsa_sel_win(bits, qkv, vst, vwt, selt, oc, gates, _block_expander(n_slc, tk), tq_sel, tk)
    return o, w_out.astype(BF16)


def kernel(x, c, diff_w_in, diff_w_out, diff_lam_q1, diff_lam_k1, diff_lam_q2, diff_lam_k2, diff_subln_g,
           nsa_w_in, nsa_cmp_pos_k, nsa_cmp_pos_v, nsa_cmp_k_w1, nsa_cmp_k_w2, nsa_cmp_v_w1, nsa_cmp_v_w2,
           nsa_w_out, ada_w, ada_b, ln_mix_g, ln_mix_b, ln_ffn_g, ln_ffn_b, ffn_w_gu, ffn_w_down):
    mod = _ada_modulation(c, ada_w, ada_b)
    for i in range(DEPTH):
        j = i // N_MIXERS
        if i % N_MIXERS == 0:
            o, w_out = _diff_mixer(x, mod, i, diff_w_in[j], diff_w_out[j], diff_lam_q1[j], diff_lam_k1[j],
                                   diff_lam_q2[j], diff_lam_k2[j], diff_subln_g[j])
        else:
            o, w_out = _nsa_mixer(x, mod, i, nsa_w_in[j], nsa_cmp_pos_k[j], nsa_cmp_pos_v[j],
                                  nsa_cmp_k_w1[j], nsa_cmp_k_w2[j], nsa_cmp_v_w1[j], nsa_cmp_v_w2[j],
                                  nsa_w_out[j])
        x = _mix_ffn(o, w_out, x, mod, i, ln_mix_g[i], ln_mix_b[i],
                     ffn_w_gu[i].astype(BF16), ffn_w_down[i].astype(BF16), ln_ffn_g[i], ln_ffn_b[i])
    return x
```
